```python
import math
import jax, jax.numpy as jnp
from jax import lax
import numpy as np

D_MODEL = 2048
BATCH = 2
SEQ = 8192
DEPTH = 2

N_MEM = 256
EXPAND = 2
D_MIX = EXPAND * D_MODEL
N_GROUPS = 4
D_GROUP = D_MIX // N_GROUPS
CONV_WIDTH = 31
SHORT_CONV_WIDTH = 3
HYENA_ORDER = 2
HYENA_DIRS = 2
HYENA_BANDS = 16
HYENA_EMB = 1 + 2 * HYENA_BANDS
HYENA_FFN = 64
HYENA_FAST_DECAY = -math.log(1e-2) / 0.3
HYENA_SLOW_DECAY = -math.log(1e-2) / 1.5
MEM_HEADS = 4
MEM_HEAD_DIM = D_GROUP // MEM_HEADS
N_IN = 2 * D_GROUP + D_GROUP + (HYENA_ORDER + 1) * D_GROUP + D_GROUP + D_MIX
EPS = 1e-6

kernel_name = "hybrid_conv_fourier_hyena_memattn_encoder"


def rms_norm(x, g):
    xf = x.astype(jnp.float32)
    y = xf * lax.rsqrt(jnp.mean(xf * xf, axis=-1, keepdims=True) + EPS)
    return (y * g.astype(jnp.float32)).astype(x.dtype)


def layer_norm(x, g, b):
    xf = x.astype(jnp.float32)
    mu = jnp.mean(xf, axis=-1, keepdims=True)
    var = jnp.mean(jnp.square(xf - mu), axis=-1, keepdims=True)
    y = (xf - mu) * lax.rsqrt(var + EPS)
    return (y * g.astype(jnp.float32) + b.astype(jnp.float32)).astype(x.dtype)


def depthwise_conv(u, k, b):
    y = lax.conv_general_dilated(
        u, k[:, None, :].astype(u.dtype), window_strides=(1,), padding="SAME",
        dimension_numbers=("NWC", "WIO", "NWC"), feature_group_count=u.shape[-1])
    return y + b.astype(u.dtype)


def hyena_filters(positions, fw1, fb1, freq1, fw2, fb2, freq2, fw3, decay):
    f32 = jnp.float32
    L = positions.shape[0]
    pos = positions.astype(f32)
    t = pos / L
    bands = jnp.linspace(1e-4, HYENA_BANDS - 1, HYENA_BANDS, dtype=f32)
    w = 2.0 * jnp.pi * pos / L
    feats = jnp.concatenate([t[:, None], jnp.cos(w[:, None] * bands),
                             jnp.sin(w[:, None] * bands)], axis=-1)
    h = jnp.sin(freq1.astype(f32) * (feats @ fw1.astype(f32) + fb1.astype(f32)))
    h = jnp.sin(freq2.astype(f32) * (h @ fw2.astype(f32) + fb2.astype(f32)))
    h = (h @ fw3.astype(f32)).reshape(L, HYENA_ORDER, HYENA_DIRS, -1)
    h = h * jnp.exp(-t[:, None, None, None] * jnp.abs(decay.astype(f32)))
    fwd, bwd = h[:, :, 0], h[:, :, 1]
    k = jnp.concatenate([fwd, jnp.zeros_like(fwd[:1]), jnp.flip(bwd[1:], axis=0)], axis=0)
    k = k * lax.rsqrt(jnp.sum(k * k, axis=0, keepdims=True) + EPS)
    return jnp.fft.rfft(k, axis=0)


def long_conv(z, kf, skip):
    L = z.shape[1]
    zf = jnp.fft.rfft(z.astype(jnp.float32), n=2 * L, axis=1)
    y = jnp.fft.irfft(zf * kf[None], n=2 * L, axis=1)[:, :L]
    return (y + skip.astype(jnp.float32) * z.astype(jnp.float32)).astype(z.dtype)


def hybrid_layer(x, mem_n, positions, pre_g, post_g, w_in, conv_dw_w, conv_dw_b, conv_ln_g,
                 conv_ln_b, conv_pw_w, conv_pw_b, fnet_w, fnet_b, hy_short_w, hy_short_b,
                 hy_fw1, hy_fb1, hy_freq1, hy_fw2, hy_fb2, hy_freq2, hy_fw3, hy_decay, hy_skip,
                 mem_wk, mem_wv, group_g, w_out):
    B, L, _ = x.shape
    G = D_GROUP
    h = rms_norm(x, pre_g)
    proj = h @ w_in.astype(x.dtype)
    a, f, hy, q, gate = jnp.split(proj, [2 * G, 3 * G, 6 * G, 7 * G], axis=-1)

    a_val, a_gate = jnp.split(a, 2, axis=-1)
    u = a_val * jax.nn.sigmoid(a_gate)
    u = depthwise_conv(u, conv_dw_w, conv_dw_b)
    u = jax.nn.silu(layer_norm(u, conv_ln_g, conv_ln_b))
    y_a = u @ conv_pw_w.astype(x.dtype) + conv_pw_b.astype(x.dtype)

    ff = jnp.fft.fft2(f.astype(jnp.float32), axes=(1, 2), norm="ortho").real.astype(x.dtype)
    y_b = ff @ fnet_w.astype(x.dtype) + fnet_b.astype(x.dtype)

    hy = depthwise_conv(hy, hy_short_w, hy_short_b)
    v, x1, x2 = jnp.split(hy, 3, axis=-1)
    kf = hyena_filters(positions, hy_fw1, hy_fb1, hy_freq1, hy_fw2, hy_fb2, hy_freq2,
                       hy_fw3, hy_decay)
    z = v
    for o, g_o in enumerate((x1, x2)):
        z = g_o * long_conv(z, kf[:, o], hy_skip[o])
    y_c = z

    qh = q.reshape(B, L, MEM_HEADS, MEM_HEAD_DIM)
    kh = (mem_n @ mem_wk.astype(x.dtype)).reshape(B, -1, MEM_HEADS, MEM_HEAD_DIM)
    vh = (mem_n @ mem_wv.astype(x.dtype)).reshape(B, -1, MEM_HEADS, MEM_HEAD_DIM)
    s = jnp.einsum("bqhd,bkhd->bhqk", qh, kh).astype(jnp.float32) * (MEM_HEAD_DIM ** -0.5)
    p = jax.nn.softmax(s, axis=-1).astype(x.dtype)
    y_m = jnp.einsum("bhqk,bkhd->bqhd", p, vh).reshape(B, L, G)

    y = jnp.concatenate([y_a, y_b, y_c, y_m], axis=-1).reshape(B, L, N_GROUPS, G)
    y = rms_norm(y, group_g.reshape(N_GROUPS, G)).reshape(B, L, D_MIX)
    y = y * jax.nn.silu(gate)
    out = y @ w_out.astype(x.dtype)
    return x + rms_norm(out, post_g)


def setup_inputs(seed: int = 0) -> dict:
    key = jax.random.key(seed)
    ks = iter(jax.random.split(key, 40))
    f32 = jnp.float32

    def nrm(shape, scale):
        return scale * jax.random.normal(next(ks), shape, f32)

    def gain(shape):
        return 1.0 + nrm(shape, 0.05)

    G = D_GROUP
    decay_base = jnp.linspace(HYENA_SLOW_DECAY, HYENA_FAST_DECAY, G, dtype=f32)
    return {
        "x": nrm((BATCH, SEQ, D_MODEL), 1.0),
        "mem": nrm((BATCH, N_MEM, D_MODEL), 1.0),
        "positions": jnp.arange(SEQ, dtype=jnp.int32),
        "mem_norm_g": gain((D_MODEL,)),
        "pre_norm_g": gain((DEPTH, D_MODEL)),
        "post_norm_g": gain((DEPTH, D_MODEL)),
        "w_in": nrm((DEPTH, D_MODEL, N_IN), D_MODEL ** -0.5),
        "conv_dw_w": nrm((DEPTH, CONV_WIDTH, G), CONV_WIDTH ** -0.5),
        "conv_dw_b": nrm((DEPTH, G), 0.02),
        "conv_ln_g": gain((DEPTH, G)),
        "conv_ln_b": nrm((DEPTH, G), 0.02),
        "conv_pw_w": nrm((DEPTH, G, G), G ** -0.5),
        "conv_pw_b": nrm((DEPTH, G), 0.02),
        "fnet_w": nrm((DEPTH, G, G), G ** -0.5),
        "fnet_b": nrm((DEPTH, G), 0.02),
        "hy_short_w": nrm((DEPTH, SHORT_CONV_WIDTH, 3 * G), SHORT_CONV_WIDTH ** -0.5),
        "hy_short_b": nrm((DEPTH, 3 * G), 0.02),
        "hy_fw1": nrm((DEPTH, HYENA_EMB, HYENA_FFN), HYENA_EMB ** -0.5),
        "hy_fb1": nrm((DEPTH, HYENA_FFN), 0.1),
        "hy_freq1": gain((DEPTH, HYENA_FFN)),
        "hy_fw2": nrm((DEPTH, HYENA_FFN, HYENA_FFN), HYENA_FFN ** -0.5),
        "hy_fb2": nrm((DEPTH, HYENA_FFN), 0.1),
        "hy_freq2": gain((DEPTH, HYENA_FFN)),
        "hy_fw3": nrm((DEPTH, HYENA_FFN, HYENA_ORDER * HYENA_DIRS * G), HYENA_FFN ** -0.5),
        "hy_decay": decay_base * (1.0 + nrm((DEPTH, HYENA_ORDER, HYENA_DIRS, G), 0.1)),
        "hy_skip": nrm((DEPTH, HYENA_ORDER, G), 0.5),
        "mem_wk": nrm((DEPTH, D_MODEL, G), D_MODEL ** -0.5),
        "mem_wv": nrm((DEPTH, D_MODEL, G), D_MODEL ** -0.5),
        "group_norm_g": gain((DEPTH, D_MIX)),
        "w_out": nrm((DEPTH, D_MIX, D_MODEL), D_MIX ** -0.5),
    }


def reference(x, mem, positions, mem_norm_g, pre_norm_g, post_norm_g, w_in, conv_dw_w,
              conv_dw_b, conv_ln_g, conv_ln_b, conv_pw_w, conv_pw_b, fnet_w, fnet_b,
              hy_short_w, hy_short_b, hy_fw1, hy_fb1, hy_freq1, hy_fw2, hy_fb2, hy_freq2,
              hy_fw3, hy_decay, hy_skip, mem_wk, mem_wv, group_norm_g, w_out):
    mem_n = rms_norm(mem, mem_norm_g).astype(x.dtype)
    for l in range(DEPTH):
        x = hybrid_layer(x, mem_n, positions, pre_norm_g[l], post_norm_g[l], w_in[l],
                         conv_dw_w[l], conv_dw_b[l], conv_ln_g[l], conv_ln_b[l], conv_pw_w[l],
                         conv_pw_b[l], fnet_w[l], fnet_b[l], hy_short_w[l], hy_short_b[l],
                         hy_fw1[l], hy_fb1[l], hy_freq1[l], hy_fw2[l], hy_fb2[l], hy_freq2[l],
                         hy_fw3[l], hy_decay[l], hy_skip[l], mem_wk[l], mem_wv[l],
                         group_norm_g[l], w_out[l])
    return x
```

```python
import functools
import math

import jax
import jax.numpy as jnp
from jax import lax
from jax.experimental import pallas as pl
from jax.experimental.pallas import tpu as pltpu

BF = jnp.bfloat16
F32 = jnp.float32
EPS = 1e-6

CONV_WIDTH = 31
CONV_HALF = CONV_WIDTH // 2
HALO = 16
CONV_STAGE_ROWS = 64
HYENA_BANDS = 16
HYENA_ORDER = 2
MEM_HEADS = 4
FFT_INNER = 128
FEAT_PAD = 128
MASK_COL = 2 * HYENA_BANDS + 1
VMEM_LIMIT = 56 * 1024 * 1024


def _params(*sem):
    return pltpu.CompilerParams(dimension_semantics=sem, vmem_limit_bytes=VMEM_LIMIT)


def _rmsnorm_kernel(x_ref, g_ref, o_ref):
    x = x_ref[...]
    ms = jnp.mean(x * x, axis=-1, keepdims=True)
    o_ref[...] = (x * lax.rsqrt(ms + EPS) * g_ref[...]).astype(o_ref.dtype)


def rmsnorm_bf16(x, g, tm):
    m, d = x.shape
    return pl.pallas_call(
        _rmsnorm_kernel,
        grid=(m // tm,),
        in_specs=[pl.BlockSpec((tm, d), lambda i: (i, 0)),
                  pl.BlockSpec((1, d), lambda i: (0, 0))],
        out_specs=pl.BlockSpec((tm, d), lambda i: (i, 0)),
        out_shape=jax.ShapeDtypeStruct((m, d), BF),
        compiler_params=_params("parallel"),
        name="rmsnorm",
    )(x, g.reshape(1, d))


def _mm_kernel(a_ref, w_ref, o_ref):
    o_ref[...] = jnp.dot(a_ref[...], w_ref[...], preferred_element_type=F32).astype(o_ref.dtype)


def matmul(a, w, tm, tn, name):
    m, k = a.shape
    n = w.shape[1]
    tm, tn = min(tm, m), min(tn, n)
    return pl.pallas_call(
        _mm_kernel,
        grid=(m // tm, n // tn),
        in_specs=[pl.BlockSpec((tm, k), lambda i, j: (i, 0)),
                  pl.BlockSpec((k, tn), lambda i, j: (0, j))],
        out_specs=pl.BlockSpec((tm, tn), lambda i, j: (i, j)),
        out_shape=jax.ShapeDtypeStruct((m, n), BF),
        compiler_params=_params("parallel", "arbitrary"),
        name=name,
    )(a, w)


def _halo_specs(tr, width, col):
    per = tr // HALO

    def cur(i):
        return (i, col)

    def prev(i):
        return (jnp.maximum(i * per - 1, 0), col)

    def make_next(nblocks):
        def nxt(i):
            return (jnp.minimum((i + 1) * per, nblocks - 1), col)
        return nxt

    return cur, prev, make_next


def _seq_edges(i, tr, seq):
    tiles = seq // tr
    pos = i % tiles
    return pos == 0, pos == tiles - 1


def _conva_kernel(vc, gc, vp, gp, vn, gn, dww, dwb, lng, lnb, pww, pwb, o_ref, upad, cbuf, stage, *, seq):
    tr, c = vc.shape
    first, last = _seq_edges(pl.program_id(0), tr, seq)

    def glu(v, g):
        return v[...].astype(F32) * jax.nn.sigmoid(g[...].astype(F32))

    upad[pl.ds(HALO, tr), :] = glu(vc, gc)
    upad[pl.ds(0, HALO), :] = jnp.where(first, 0.0, glu(vp, gp))
    upad[pl.ds(HALO + tr, HALO), :] = jnp.where(last, 0.0, glu(vn, gn))

    rb, rc = stage.shape[0] - 2 * HALO, 16

    def body(r, carry):
        r0 = pl.multiple_of(r * rb, rb)
        stage[...] = upad[pl.ds(r0, rb + 2 * HALO), :]
        for q in range(rb // rc):
            acc = jnp.broadcast_to(dwb[...], (rc, c))
            for j in range(CONV_WIDTH):
                acc = acc + stage[pl.ds(q * rc + HALO - CONV_HALF + j, rc), :] * dww[pl.ds(j, 1), :]
            cbuf[pl.ds(pl.multiple_of(r0 + q * rc, rc), rc), :] = acc
        return carry

    lax.fori_loop(0, tr // rb, body, 0)

    u = cbuf[...]
    mu = jnp.mean(u, axis=-1, keepdims=True)
    d = u - mu
    var = jnp.mean(d * d, axis=-1, keepdims=True)
    y = d * lax.rsqrt(var + EPS) * lng[...] + lnb[...]
    y = y * jax.nn.sigmoid(y)
    out = jnp.dot(y.astype(BF), pww[...], preferred_element_type=F32) + pwb[...]
    o_ref[...] = out.astype(o_ref.dtype)


def conv_module(a, dww, dwb, lng, lnb, pww, pwb, seq, tr):
    m, two_c = a.shape
    c = two_c // 2
    cur, prev, make_next = _halo_specs(tr, c, 0)
    cur_g, prev_g, make_next_g = _halo_specs(tr, c, 1)
    nh = m // HALO
    row = lambda i: (0, 0)
    return pl.pallas_call(
        functools.partial(_conva_kernel, seq=seq),
        grid=(m // tr,),
        in_specs=[pl.BlockSpec((tr, c), cur), pl.BlockSpec((tr, c), cur_g),
                  pl.BlockSpec((HALO, c), prev), pl.BlockSpec((HALO, c), prev_g),
                  pl.BlockSpec((HALO, c), make_next(nh)), pl.BlockSpec((HALO, c), make_next_g(nh)),
                  pl.BlockSpec((CONV_WIDTH, c), row), pl.BlockSpec((1, c), row),
                  pl.BlockSpec((1, c), row), pl.BlockSpec((1, c), row),
                  pl.BlockSpec((c, c), row), pl.BlockSpec((1, c), row)],
        out_specs=pl.BlockSpec((tr, c), lambda i: (i, 0)),
        out_shape=jax.ShapeDtypeStruct((m, c), BF),
        scratch_shapes=[pltpu.VMEM((tr + 2 * HALO, c), F32), pltpu.VMEM((tr, c), F32),
                        pltpu.VMEM((CONV_STAGE_ROWS + 2 * HALO, c), F32)],
        compiler_params=_params("parallel"),
        name="conv_module",
    )(a, a, a, a, a, a, dww, dwb.reshape(1, c), lng.reshape(1, c), lnb.reshape(1, c),
      pww.astype(BF), pwb.reshape(1, c))


def _short_kernel(hc, hp, hn, w, b, v_ref, x1_ref, x2_ref, upad, *, seq):
    tr, c3 = hc.shape
    c = c3 // 3
    first, last = _seq_edges(pl.program_id(0), tr, seq)
    upad[pl.ds(HALO, tr), :] = hc[...].astype(F32)
    upad[pl.ds(0, HALO), :] = jnp.where(first, 0.0, hp[...].astype(F32))
    upad[pl.ds(HALO + tr, HALO), :] = jnp.where(last, 0.0, hn[...].astype(F32))
    for g, o_ref in enumerate((v_ref, x1_ref, x2_ref)):
        cols = pl.ds(g * c, c)
        y = (upad[pl.ds(HALO - 1, tr), cols] * w[pl.ds(0, 1), cols]
             + upad[pl.ds(HALO, tr), cols] * w[pl.ds(1, 1), cols]
             + upad[pl.ds(HALO + 1, tr), cols] * w[pl.ds(2, 1), cols] + b[:, cols])
        o_ref[...] = y.astype(o_ref.dtype)


def hyena_short_conv(hy, w, b, seq, tr):
    m, c3 = hy.shape
    c = c3 // 3
    cur, prev, make_next = _halo_specs(tr, c3, 0)
    out = jax.ShapeDtypeStruct((m, c), BF)
    ospec = pl.BlockSpec((tr, c), lambda i: (i, 0))
    return pl.pallas_call(
        functools.partial(_short_kernel, seq=seq),
        grid=(m // tr,),
        in_specs=[pl.BlockSpec((tr, c3), cur), pl.BlockSpec((HALO, c3), prev),
                  pl.BlockSpec((HALO, c3), make_next(m // HALO)),
                  pl.BlockSpec((3, c3), lambda i: (0, 0)), pl.BlockSpec((1, c3), lambda i: (0, 0))],
        out_specs=[ospec, ospec, ospec],
        out_shape=[out, out, out],
        scratch_shapes=[pltpu.VMEM((tr + 2 * HALO, c3), F32)],
        compiler_params=_params("parallel"),
        name="hyena_short_conv",
    )(hy, hy, hy, w, b.reshape(1, c3))


def _cis(p, n, sign):
    ang = (2.0 * math.pi / n) * (p % n).astype(F32)
    return jnp.cos(ang), sign * jnp.sin(ang)


def _stack_complex(mr, mi):
    return jnp.concatenate([jnp.concatenate([mr, -mi], axis=-1),
                            jnp.concatenate([mi, mr], axis=-1)], axis=-2)


def _dft1_kernel(x_ref, t_ref, or_ref, oi_ref, *, nb, c):
    for j in range(nb):
        cols = pl.ds(j * c, c)
        y = jnp.dot(t_ref[j], x_ref[:, cols], preferred_element_type=F32)
        h = y.shape[0] // 2
        or_ref[:, cols] = y[:h].astype(or_ref.dtype)
        oi_ref[:, cols] = y[h:].astype(oi_ref.dtype)


def dft_outer_stage(x, table, c, nb, name):
    bsz, rows, cols = x.shape
    n_inner, two_k, _ = table.shape
    kout = two_k // 2
    nb = min(nb, n_inner)
    out = jax.ShapeDtypeStruct((bsz, kout, cols), BF)
    ospec = pl.BlockSpec((None, kout, nb * c), lambda b, j: (b, 0, j))
    return pl.pallas_call(
        functools.partial(_dft1_kernel, nb=nb, c=c),
        grid=(bsz, n_inner // nb),
        in_specs=[pl.BlockSpec((None, rows, nb * c), lambda b, j: (b, 0, j)),
                  pl.BlockSpec((nb, two_k, rows), lambda b, j: (j, 0, 0))],
        out_specs=[ospec, ospec],
        out_shape=[out, out],
        compiler_params=_params("parallel", "arbitrary"),
        name=name,
    )(x, table)


def _filter_dft2_kernel(ur, ui, d, ss, kr_ref, ki_ref, *, kb, n2):
    scale = lax.rsqrt(ss[...] + EPS)
    for k in range(kb):
        rows = pl.ds(k * n2, n2)
        cat = jnp.concatenate([ur[rows, :], ui[rows, :]], axis=0)
        x = jnp.dot(d[...], cat, preferred_element_type=F32) * scale
        kr_ref[rows, :] = x[:n2].astype(kr_ref.dtype)
        ki_ref[rows, :] = x[n2:].astype(ki_ref.dtype)


def filter_inner_stage(ur, ui, d, ss, n2, kb):
    n, c = ur.shape
    kb = min(kb, n // n2)
    spec = pl.BlockSpec((kb * n2, c), lambda i: (i, 0))
    out = jax.ShapeDtypeStruct((n, c), BF)
    return pl.pallas_call(
        functools.partial(_filter_dft2_kernel, kb=kb, n2=n2),
        grid=(n // (kb * n2),),
        in_specs=[spec, spec, pl.BlockSpec((2 * n2, 2 * n2), lambda i: (0, 0)),
                  pl.BlockSpec((1, c), lambda i: (0, 0))],
        out_specs=[spec, spec],
        out_shape=[out, out],
        compiler_params=_params("parallel"),
        name="filter_inner_dft",
    )(ur, ui, d, ss)


def _hyena_mid_kernel(ur, ui, kr, ki, d, minv, qr_ref, qi_ref, *, kb, n2):
    for k in range(kb):
        rows = pl.ds(k * n2, n2)
        cat = jnp.concatenate([ur[rows, :], ui[rows, :]], axis=0)
        x = jnp.dot(d[...], cat, preferred_element_type=F32)
        xr, xi = x[:n2], x[n2:]
        fr, fi = kr[rows, :].astype(F32), ki[rows, :].astype(F32)
        pr = xr * fr - xi * fi
        pi = xr * fi + xi * fr
        catp = jnp.concatenate([pr, pi], axis=0).astype(BF)
        q = jnp.dot(minv[k], catp, preferred_element_type=F32)
        qr_ref[rows, :] = q[:n2].astype(qr_ref.dtype)
        qi_ref[rows, :] = q[n2:].astype(qi_ref.dtype)


def hyena_mid_stage(ur, ui, kfr, kfi, d, minv, order, n2, kb):
    n, c = ur.shape
    n1 = n // n2
    kb = min(kb, n1)
    spec = pl.BlockSpec((kb * n2, c), lambda i: (i, 0))
    kspec = pl.BlockSpec((kb * n2, c), lambda i: (i, order))
    out = jax.ShapeDtypeStruct((n, c), BF)
    return pl.pallas_call(
        functools.partial(_hyena_mid_kernel, kb=kb, n2=n2),
        grid=(n1 // kb,),
        in_specs=[spec, spec, kspec, kspec,
                  pl.BlockSpec((2 * n2, 2 * n2), lambda i: (0, 0)),
                  pl.BlockSpec((kb, 2 * n2, 2 * n2), lambda i: (i, 0, 0))],
        out_specs=[spec, spec],
        out_shape=[out, out],
        compiler_params=_params("parallel"),
        name="hyena_mid",
    )(ur, ui, kfr, kfi, d, minv)


def _hyena_out_kernel(qr, qi, g, xo, z, skip, o_ref):
    cat = jnp.concatenate([qr[...], qi[...]], axis=0)
    y = jnp.dot(g[...], cat, preferred_element_type=F32)
    zf = z[...].astype(F32)
    o_ref[...] = (xo[...].astype(F32) * (y + skip[...] * zf)).astype(o_ref.dtype)


def hyena_out_stage(qr, qi, g, xo, z, skip_tiled, tc):
    rows, cols = qr.shape
    spec = pl.BlockSpec((rows, tc), lambda j: (0, j))
    return pl.pallas_call(
        _hyena_out_kernel,
        grid=(cols // tc,),
        in_specs=[spec, spec, pl.BlockSpec((rows, 2 * rows), lambda j: (0, 0)), spec, spec,
                  pl.BlockSpec((1, tc), lambda j: (0, 0))],
        out_specs=spec,
        out_shape=jax.ShapeDtypeStruct((rows, cols), BF),
        compiler_params=_params("parallel"),
        name="hyena_out",
    )(qr, qi, g, xo, z, skip_tiled)


def _filter_kernel(ft, w1, b1, f1, w2, b2, f2, w3, dec, k_ref, ss_ref):
    hi = lax.Precision.HIGHEST
    x = ft[...]
    h = jnp.sin(f1[...] * (jnp.dot(x, w1[...], precision=hi, preferred_element_type=F32) + b1[...]))
    h = jnp.sin(f2[...] * (jnp.dot(h, w2[...], precision=hi, preferred_element_type=F32) + b2[...]))
    h = jnp.dot(h, w3[...], precision=hi, preferred_element_type=F32)
    t = x[:, 0:1]
    mask = x[:, MASK_COL:MASK_COL + 1]
    k = h * jnp.exp(-t * jnp.abs(dec[...])) * mask
    k_ref[...] = k.astype(k_ref.dtype)
    part = jnp.sum(k * k, axis=0, keepdims=True)

    @pl.when(pl.program_id(1) == 0)
    def _():
        ss_ref[...] = jnp.zeros_like(ss_ref)

    ss_ref[...] += jnp.broadcast_to(part, ss_ref.shape)


def hyena_filter_taps(feats, w1, b1, f1, w2, b2, f2, w3, dec, tr):
    n, fp = feats.shape
    ffn = w1.shape[1]
    c = w3.shape[-1]
    half = n // (2 * tr)
    small = lambda shape: pl.BlockSpec(shape, lambda o, i: (0,) * len(shape))
    taps, ss = pl.pallas_call(
        _filter_kernel,
        grid=(HYENA_ORDER, n // tr),
        in_specs=[pl.BlockSpec((tr, fp), lambda o, i: (i, 0)),
                  small((fp, ffn)), small((1, ffn)), small((1, ffn)),
                  small((ffn, ffn)), small((1, ffn)), small((1, ffn)),
                  pl.BlockSpec((None, ffn, c), lambda o, i: (2 * o + (i >= half).astype(jnp.int32), 0, 0)),
                  pl.BlockSpec((None, 1, c), lambda o, i: (2 * o + (i >= half).astype(jnp.int32), 0, 0))],
        out_specs=[pl.BlockSpec((tr, c), lambda o, i: (i, o)),
                   pl.BlockSpec((8, c), lambda o, i: (o, 0))],
        out_shape=[jax.ShapeDtypeStruct((n, HYENA_ORDER * c), BF),
                   jax.ShapeDtypeStruct((HYENA_ORDER * 8, c), F32)],
        compiler_params=_params("arbitrary", "arbitrary"),
        name="hyena_filter_taps",
    )(feats, w1, b1, f1, w2, b2, f2, w3, dec)
    return taps, ss


def _fnet_kernel(yr, yi, d2, csg, w, b, o_ref, xcat, *, kb, n2, c):
    for k in range(kb):
        rows = pl.ds(k * n2, n2)
        cat = jnp.concatenate([yr[rows, :], yi[rows, :]], axis=0)
        x = jnp.dot(d2[...], cat, preferred_element_type=F32)
        xcat[rows, pl.ds(0, c)] = x[:n2].astype(BF)
        xcat[rows, pl.ds(c, c)] = x[n2:].astype(BF)
    ff = jnp.dot(xcat[...], csg[...], preferred_element_type=F32)
    out = jnp.dot(ff.astype(BF), w[...], preferred_element_type=F32) + b[...]
    for k in range(kb):
        o_ref[:, pl.ds(k * c, c)] = out[k * n2:(k + 1) * n2].astype(o_ref.dtype)


def fnet_inner_stage(yr, yi, d2, csg, w, b, n2, kb):
    bsz, l, c = yr.shape
    n1 = l // n2
    kb = min(kb, n1)
    spec = pl.BlockSpec((None, kb * n2, c), lambda bb, i: (bb, i, 0))
    const = lambda shape: pl.BlockSpec(shape, lambda bb, i: (0,) * len(shape))
    return pl.pallas_call(
        functools.partial(_fnet_kernel, kb=kb, n2=n2, c=c),
        grid=(bsz, n1 // kb),
        in_specs=[spec, spec, const((2 * n2, 2 * n2)), const((2 * c, c)), const((c, c)), const((1, c))],
        out_specs=pl.BlockSpec((None, n2, kb * c), lambda bb, i: (bb, 0, i)),
        out_shape=jax.ShapeDtypeStruct((bsz, n2, n1 * c), BF),
        scratch_shapes=[pltpu.VMEM((kb * n2, 2 * c), BF)],
        compiler_params=_params("parallel", "arbitrary"),
        name="fnet_inner",
    )(yr, yi, d2, csg, w, b)


def _attn_kernel(q_ref, k_ref, v_ref, o_ref, *, heads):
    c = q_ref.shape[1]
    hd = c // heads
    scale = hd ** -0.5
    for h in range(heads):
        cols = pl.ds(h * hd, hd)
        s = lax.dot_general(q_ref[:, cols], k_ref[:, cols], (((1,), (1,)), ((), ())),
                            preferred_element_type=F32) * scale
        s = s - jnp.max(s, axis=-1, keepdims=True)
        e = jnp.exp(s)
        p = e / jnp.sum(e, axis=-1, keepdims=True)
        o = jnp.dot(p.astype(BF), v_ref[:, cols], preferred_element_type=F32)
        o_ref[:, cols] = o.astype(o_ref.dtype)


def memory_attention(q, kh, vh, seq, tr):
    m, c = q.shape
    _, nm, _ = kh.shape
    per = seq // tr
    kv = pl.BlockSpec((None, nm, c), lambda i: (i // per, 0, 0))
    return pl.pallas_call(
        functools.partial(_attn_kernel, heads=MEM_HEADS),
        grid=(m // tr,),
        in_specs=[pl.BlockSpec((tr, c), lambda i: (i, 0)), kv, kv],
        out_specs=pl.BlockSpec((tr, c), lambda i: (i, 0)),
        out_shape=jax.ShapeDtypeStruct((m, c), BF),
        compiler_params=_params("parallel"),
        name="memory_attention",
    )(q, kh, vh)


def _merge_kernel(ya, yb, yc, ym, gate, gg, w, x, pg, o_ref, acc):
    k = pl.program_id(1)

    def contribution(y_ref):
        y = y_ref[...].astype(F32)
        ms = jnp.mean(y * y, axis=-1, keepdims=True)
        yn = y * lax.rsqrt(ms + EPS) * gg[...]
        g = gate[...].astype(F32)
        yn = yn * (g * jax.nn.sigmoid(g))
        return jnp.dot(yn.astype(BF), w[...], preferred_element_type=F32)

    for idx, y_ref in enumerate((ya, yb, yc, ym)):
        @pl.when(k == idx)
        def _(idx=idx, y_ref=y_ref):
            if idx == 0:
                acc[...] = contribution(y_ref)
            else:
                acc[...] += contribution(y_ref)

    @pl.when(k == 3)
    def _():
        out = acc[...]
        ms = jnp.mean(out * out, axis=-1, keepdims=True)
        o_ref[...] = x[...] + out * lax.rsqrt(ms + EPS) * pg[...]


def merge_project(ya, yb, yc, ym, gate, gg, w_out, x, pg, tm):
    m, g = ya.shape
    d = x.shape[1]
    yspec = pl.BlockSpec((tm, g), lambda i, k: (i, 0))
    return pl.pallas_call(
        _merge_kernel,
        grid=(m // tm, 4),
        in_specs=[yspec, yspec, yspec, yspec,
                  pl.BlockSpec((tm, g), lambda i, k: (i, k)),
                  pl.BlockSpec((1, g), lambda i, k: (0, k)),
                  pl.BlockSpec((g, d), lambda i, k: (k, 0)),
                  pl.BlockSpec((tm, d), lambda i, k: (i, 0)),
                  pl.BlockSpec((1, d), lambda i, k: (0, 0))],
        out_specs=pl.BlockSpec((tm, d), lambda i, k: (i, 0)),
        out_shape=jax.ShapeDtypeStruct((m, d), F32),
        scratch_shapes=[pltpu.VMEM((tm, d), F32)],
        compiler_params=_params("parallel", "arbitrary"),
        name="merge_project",
    )(ya, yb, yc, ym, gate, gg, w_out, x, pg)


def _iota(n):
    return jnp.arange(n, dtype=jnp.int32)


def _outer_tables(n1, n2, rows_complex):
    n = n1 * n2
    r = _iota(n2)[:, None, None]
    k1 = _iota(n1)[None, :, None]
    nj = n1 // 2 if rows_complex else n1
    j = _iota(nj)[None, None, :]
    er, ei = _cis(k1 * (n2 * j + r), n, -1.0)
    if rows_complex:
        return _stack_complex(er, ei).astype(BF)
    return jnp.concatenate([er, ei], axis=-2).astype(BF)


def _inner_table(n2):
    k = _iota(n2)[:, None]
    j = _iota(n2)[None, :]
    cr, ci = _cis(k * j, n2, -1.0)
    return _stack_complex(cr, ci).astype(BF)


def _inverse_inner_tables(n1, n2):
    n = n1 * n2
    k1 = _iota(n1)[:, None, None]
    na = _iota(n2)[None, :, None]
    k2 = _iota(n2)[None, None, :]
    ar, ai = _cis(na * (k1 + n1 * k2), n, 1.0)
    return _stack_complex(ar, ai).astype(BF)


def _inverse_outer_table(n1, n2):
    nb = _iota(n1 // 2)[:, None]
    k1 = _iota(n1)[None, :]
    gr, gi = _cis(nb * k1, n1, 1.0)
    inv = 1.0 / (n1 * n2)
    return _stack_complex(gr * inv, gi * inv).astype(BF)


def _channel_table(g, scale):
    a = _iota(g)[:, None]
    b = _iota(g)[None, :]
    cg, sg = _cis(a * b, g, 1.0)
    return (jnp.concatenate([cg, sg], axis=0) * scale).astype(BF)


def _filter_features(positions, seq):
    idx = _iota(2 * seq)
    src = jnp.where(idx < seq, idx, 2 * seq - idx)
    src = jnp.where(idx == seq, 0, src)
    pos = positions.astype(F32)[src]
    t = pos / seq
    bands = jnp.linspace(1e-4, HYENA_BANDS - 1, HYENA_BANDS, dtype=F32)
    w = 2.0 * jnp.pi * pos / seq
    mask = (idx != seq).astype(F32)
    feats = jnp.concatenate([t[:, None], jnp.cos(w[:, None] * bands), jnp.sin(w[:, None] * bands),
                             mask[:, None]], axis=-1)
    return jnp.pad(feats, ((0, 0), (0, FEAT_PAD - feats.shape[1])))


def _layer(x2, mem_n, feats, tabs, p, dims):
    bsz, seq, d, g = dims
    m = bsz * seq
    n2 = FFT_INNER
    n1 = 2 * seq // n2
    n1f, n2f = FFT_INNER, seq // FFT_INNER

    h = rmsnorm_bf16(x2, p["pre_g"], 512)
    w_in = p["w_in"].astype(BF)
    tm, tn = 1024, 1024
    a = matmul(h, w_in[:, 0:2 * g], tm, tn, "proj_a")
    f = matmul(h, w_in[:, 2 * g:3 * g], tm, tn, "proj_f")
    hy = matmul(h, w_in[:, 3 * g:6 * g], tm, tn, "proj_hy")
    q = matmul(h, w_in[:, 6 * g:7 * g], tm, tn, "proj_q")
    gate = matmul(h, w_in[:, 7 * g:], tm, tn, "proj_gate")

    y_a = conv_module(a, p["conv_dw_w"], p["conv_dw_b"], p["conv_ln_g"], p["conv_ln_b"],
                      p["conv_pw_w"], p["conv_pw_b"], seq, 512)

    yr, yi = dft_outer_stage(f.reshape(bsz, n1f, n2f * g), tabs["fnet_outer"], g, 8, "fnet_outer")
    y_b = fnet_inner_stage(yr.reshape(bsz, seq, g), yi.reshape(bsz, seq, g), tabs["fnet_inner"],
                           tabs["fnet_chan"], p["fnet_w"].astype(BF), p["fnet_b"].reshape(1, g), n2f, 8)
    y_b = y_b.reshape(m, g)

    v, x1, x2g = hyena_short_conv(hy, p["hy_short_w"], p["hy_short_b"], seq, 512)
    ffn = p["hy_fw1"].shape[1]
    w1 = jnp.pad(p["hy_fw1"], ((0, FEAT_PAD - p["hy_fw1"].shape[0]), (0, 0)))
    w3 = p["hy_fw3"].reshape(ffn, 2 * HYENA_ORDER, g).transpose(1, 0, 2)
    dec = p["hy_decay"].reshape(2 * HYENA_ORDER, 1, g)
    taps, ss = hyena_filter_taps(feats, w1, p["hy_fb1"].reshape(1, ffn), p["hy_freq1"].reshape(1, ffn),
                                 p["hy_fw2"], p["hy_fb2"].reshape(1, ffn), p["hy_freq2"].reshape(1, ffn),
                                 w3, dec, 512)
    og = HYENA_ORDER * g
    kur, kui = dft_outer_stage(taps.reshape(1, n1, n2 * og), tabs["filt_outer"], og, 4, "filter_outer_dft")
    ss_row = ss.reshape(HYENA_ORDER, 8, g)[:, 0, :].reshape(1, og)
    kfr, kfi = filter_inner_stage(kur.reshape(n1 * n2, og), kui.reshape(n1 * n2, og), tabs["inner"], ss_row, n2, 4)

    z = v
    for o, gate_o in enumerate((x1, x2g)):
        zv = z.reshape(1, n1, n2 * g)
        ur, ui = dft_outer_stage(zv, tabs["hy_outer"], g, 8, "hyena_outer_dft")
        qr, qi = hyena_mid_stage(ur.reshape(n1 * n2, g), ui.reshape(n1 * n2, g), kfr, kfi,
                                 tabs["inner"], tabs["inv_inner"], o, n2, 8)
        tc = min(8 * g, n2 * g)
        skip_tiled = jnp.tile(p["hy_skip"][o].reshape(1, g), (1, tc // g))
        z = hyena_out_stage(qr.reshape(n1, n2 * g), qi.reshape(n1, n2 * g), tabs["inv_outer"],
                            gate_o.reshape(n1, n2 * g), z.reshape(n1, n2 * g), skip_tiled, tc)
    y_c = z.reshape(m, g)

    nm = mem_n.shape[0] // bsz
    kh = matmul(mem_n, p["mem_wk"].astype(BF), 512, 1024, "mem_k").reshape(bsz, nm, g)
    vh = matmul(mem_n, p["mem_wv"].astype(BF), 512, 1024, "mem_v").reshape(bsz, nm, g)
    y_m = memory_attention(q, kh, vh, seq, 1024)

    return merge_project(y_a, y_b, y_c, y_m, gate, p["group_g"].reshape(1, 4 * g), p["w_out"].astype(BF),
                         x2, p["post_g"].reshape(1, d), 512)


def kernel(x, mem, positions, mem_norm_g, pre_norm_g, post_norm_g, w_in, conv_dw_w, conv_dw_b, conv_ln_g, conv_ln_b, conv_pw_w, conv_pw_b, fnet_w, fnet_b, hy_short_w, hy_short_b, hy_fw1, hy_fb1, hy_freq1, hy_fw2, hy_fb2, hy_freq2, hy_fw3, hy_decay, hy_skip, mem_wk, mem_wv, group_norm_g, w_out):
    bsz, seq, d = x.shape
    g = conv_pw_w.shape[-1]
    depth = w_in.shape[0]
    assert bsz == 2, "the long convolution packs exactly two batch elements as one complex sequence"
    assert seq % (FFT_INNER * HALO) == 0
    n2 = FFT_INNER
    n1 = 2 * seq // n2
    n1f, n2f = FFT_INNER, seq // FFT_INNER

    tabs = {
        "fnet_outer": _outer_tables(n1f, n2f, False),
        "fnet_inner": _inner_table(n2f),
        "fnet_chan": _channel_table(g, 1.0 / math.sqrt(seq * g)),
        "hy_outer": _outer_tables(n1, n2, True),
        "filt_outer": _outer_tables(n1, n2, False),
        "inner": _inner_table(n2),
        "inv_inner": _inverse_inner_tables(n1, n2),
        "inv_outer": _inverse_outer_table(n1, n2),
    }
    feats = _filter_features(positions, seq)
    mem_n = rmsnorm_bf16(mem.reshape(-1, d), mem_norm_g, min(512, mem.shape[0] * mem.shape[1]))

    x2 = x.reshape(bsz * seq, d)
    for l in range(depth):
        p = {
            "pre_g": pre_norm_g[l], "post_g": post_norm_g[l], "w_in": w_in[l],
            "conv_dw_w": conv_dw_w[l], "conv_dw_b": conv_dw_b[l], "conv_ln_g": conv_ln_g[l],
            "conv_ln_b": conv_ln_b[l], "conv_pw_w": conv_pw_w[l], "conv_pw_b": conv_pw_b[l],
            "fnet_w": fnet_w[l], "fnet_b": fnet_b[l], "hy_short_w": hy_short_w[l], "hy_short_b": hy_short_b[l],
            "hy_fw1": hy_fw1[l], "hy_fb1": hy_fb1[l], "hy_freq1": hy_freq1[l], "hy_fw2": hy_fw2[l],
            "hy_fb2": hy_fb2[l], "hy_freq2": hy_freq2[l], "hy_fw3": hy_fw3[l], "hy_decay": hy_decay[l],
            "hy_skip": hy_skip[l], "mem_wk": mem_wk[l], "mem_wv": mem_wv[l], "group_g": group_norm_g[l],
            "w_out": w_out[l],
        }
        x2 = _layer(x2, mem_n, feats, tabs, p, (bsz, seq, d, g))
    return x2.reshape(bsz, seq, d)
```

```python
import functools
import math

import jax
import jax.numpy as jnp
from jax import lax
from jax.experimental import pallas as pl
from jax.experimental.pallas import tpu as pltpu

BF = jnp.bfloat16
F32 = jnp.float32
EPS = 1e-6

CONV_WIDTH = 31
CONV_HALF = CONV_WIDTH // 2
HALO = 16
SUBLANES, LANES = 8, 128
HYENA_BANDS = 16
HYENA_ORDER = 2
MEM_HEADS = 4
FFT_INNER = 128
FEAT_PAD = 128
FWD_COL = 2 * HYENA_BANDS + 1
BWD_COL = 2 * HYENA_BANDS + 2
DFT_RESIDUES = 16
VMEM_LIMIT = 56 * 1024 * 1024


def _params(*sem):
    return pltpu.CompilerParams(dimension_semantics=sem, vmem_limit_bytes=VMEM_LIMIT)


def _rmsnorm_kernel(x_ref, g_ref, o_ref):
    x = x_ref[...]
    ms = jnp.mean(x * x, axis=-1, keepdims=True)
    o_ref[...] = (x * lax.rsqrt(ms + EPS) * g_ref[...]).astype(o_ref.dtype)


def rmsnorm_bf16(x, g, tm):
    m, d = x.shape
    return pl.pallas_call(
        _rmsnorm_kernel,
        grid=(m // tm,),
        in_specs=[pl.BlockSpec((tm, d), lambda i: (i, 0)),
                  pl.BlockSpec((1, d), lambda i: (0, 0))],
        out_specs=pl.BlockSpec((tm, d), lambda i: (i, 0)),
        out_shape=jax.ShapeDtypeStruct((m, d), BF),
        compiler_params=_params("parallel"),
        name="rmsnorm",
    )(x, g.reshape(1, d))


def _mm_kernel(a_ref, w_ref, o_ref):
    o_ref[...] = jnp.dot(a_ref[...], w_ref[...], preferred_element_type=F32).astype(o_ref.dtype)


def matmul(a, w, tm, tn, name, col0=0, n=None):
    m, k = a.shape
    n = w.shape[1] if n is None else n
    tm, tn = min(tm, m), min(tn, n)
    joff = col0 // tn
    return pl.pallas_call(
        _mm_kernel,
        grid=(m // tm, n // tn),
        in_specs=[pl.BlockSpec((tm, k), lambda i, j: (i, 0)),
                  pl.BlockSpec((k, tn), lambda i, j: (0, j + joff))],
        out_specs=pl.BlockSpec((tm, tn), lambda i, j: (i, j)),
        out_shape=jax.ShapeDtypeStruct((m, n), BF),
        compiler_params=_params("parallel", "arbitrary"),
        name=name,
    )(a, w)


def _halo_specs(tr, width, col):
    per = tr // HALO

    def cur(i):
        return (i, col)

    def prev(i):
        return (jnp.maximum(i * per - 1, 0), col)

    def make_next(nblocks):
        def nxt(i):
            return (jnp.minimum((i + 1) * per, nblocks - 1), col)
        return nxt

    return cur, prev, make_next


def _seq_edges(i, tr, seq):
    tiles = seq // tr
    pos = i % tiles
    return pos == 0, pos == tiles - 1


def _conva_kernel(vc, gc, vp, gp, vn, gn, dww, dwb, lng, lnb, pww, pwb, o_ref, upad, cbuf, wbc, *, seq):
    tr, c = vc.shape
    first, last = _seq_edges(pl.program_id(0), tr, seq)

    def glu(v, g):
        return v[...].astype(F32) * jax.nn.sigmoid(g[...].astype(F32))

    upad[pl.ds(HALO, tr), :] = glu(vc, gc)
    upad[pl.ds(0, HALO), :] = jnp.where(first, 0.0, glu(vp, gp))
    upad[pl.ds(HALO + tr, HALO), :] = jnp.where(last, 0.0, glu(vn, gn))

    for j in range(CONV_WIDTH):
        wbc[pl.ds(SUBLANES * j, SUBLANES), :] = jnp.broadcast_to(dww[pl.ds(j, 1), :], (SUBLANES, c))
    row = lax.broadcasted_iota(jnp.int32, (SUBLANES, LANES), 0)
    for ct in range(c // LANES):
        lanes = pl.ds(ct * LANES, LANES)

        def residue_sums(b, lanes=lanes):
            u = [upad[pl.ds(pl.multiple_of((b + a) * SUBLANES, SUBLANES), SUBLANES), lanes] for a in range(4)]
            sums = []
            for s in range(SUBLANES):
                acc = None
                for a in range(4):
                    j = SUBLANES * a + s - 1
                    if 0 <= j < CONV_WIDTH:
                        term = u[a] * wbc[pl.ds(SUBLANES * j, SUBLANES), lanes]
                        acc = term if acc is None else acc + term
                sums.append(acc)
            return tuple(sums)

        bias = jnp.broadcast_to(dwb[:, lanes], (SUBLANES, LANES))

        def body(b, prev, lanes=lanes, bias=bias, residue_sums=residue_sums):
            cur = residue_sums(b)
            terms = [bias + prev[0]]
            for s in range(1, SUBLANES):
                terms.append(pltpu.roll(jnp.where(row >= s, prev[s], cur[s]), SUBLANES - s, axis=0))
            while len(terms) > 1:
                terms = [terms[i] + terms[i + 1] for i in range(0, len(terms), 2)]
            out = terms[0]
            cbuf[pl.ds(pl.multiple_of((b - 1) * SUBLANES, SUBLANES), SUBLANES), lanes] = out
            return cur

        lax.fori_loop(1, tr // SUBLANES + 1, body, residue_sums(0), unroll=4)

    u = cbuf[...]
    mu = jnp.mean(u, axis=-1, keepdims=True)
    d = u - mu
    var = jnp.mean(d * d, axis=-1, keepdims=True)
    y = d * lax.rsqrt(var + EPS) * lng[...] + lnb[...]
    y = y * jax.nn.sigmoid(y)
    out = jnp.dot(y.astype(BF), pww[...], preferred_element_type=F32) + pwb[...]
    o_ref[...] = out.astype(o_ref.dtype)


def conv_module(a, dww, dwb, lng, lnb, pww, pwb, seq, tr):
    m, two_c = a.shape
    c = two_c // 2
    cur, prev, make_next = _halo_specs(tr, c, 0)
    cur_g, prev_g, make_next_g = _halo_specs(tr, c, 1)
    nh = m // HALO
    row = lambda i: (0, 0)
    return pl.pallas_call(
        functools.partial(_conva_kernel, seq=seq),
        grid=(m // tr,),
        in_specs=[pl.BlockSpec((tr, c), cur), pl.BlockSpec((tr, c), cur_g),
                  pl.BlockSpec((HALO, c), prev), pl.BlockSpec((HALO, c), prev_g),
                  pl.BlockSpec((HALO, c), make_next(nh)), pl.BlockSpec((HALO, c), make_next_g(nh)),
                  pl.BlockSpec((CONV_WIDTH, c), row), pl.BlockSpec((1, c), row),
                  pl.BlockSpec((1, c), row), pl.BlockSpec((1, c), row),
                  pl.BlockSpec((c, c), row), pl.BlockSpec((1, c), row)],
        out_specs=pl.BlockSpec((tr, c), lambda i: (i, 0)),
        out_shape=jax.ShapeDtypeStruct((m, c), BF),
        scratch_shapes=[pltpu.VMEM((tr + 2 * HALO, c), F32), pltpu.VMEM((tr, c), F32),
                        pltpu.VMEM((CONV_WIDTH * SUBLANES, c), F32)],
        compiler_params=_params("parallel"),
        name="conv_module",
    )(a, a, a, a, a, a, dww, dwb.reshape(1, c), lng.reshape(1, c), lnb.reshape(1, c),
      pww.astype(BF), pwb.reshape(1, c))


def _short_kernel(hc, hp, hn, w, b, v_ref, x1_ref, x2_ref, upad, *, seq):
    tr, c3 = hc.shape
    c = c3 // 3
    first, last = _seq_edges(pl.program_id(0), tr, seq)
    upad[pl.ds(HALO, tr), :] = hc[...].astype(F32)
    upad[pl.ds(0, HALO), :] = jnp.where(first, 0.0, hp[...].astype(F32))
    upad[pl.ds(HALO + tr, HALO), :] = jnp.where(last, 0.0, hn[...].astype(F32))
    for g, o_ref in enumerate((v_ref, x1_ref, x2_ref)):
        cols = pl.ds(g * c, c)
        y = (upad[pl.ds(HALO - 1, tr), cols] * w[pl.ds(0, 1), cols]
             + upad[pl.ds(HALO, tr), cols] * w[pl.ds(1, 1), cols]
             + upad[pl.ds(HALO + 1, tr), cols] * w[pl.ds(2, 1), cols] + b[:, cols])
        o_ref[...] = y.astype(o_ref.dtype)


def hyena_short_conv(hy, w, b, seq, tr):
    m, c3 = hy.shape
    c = c3 // 3
    cur, prev, make_next = _halo_specs(tr, c3, 0)
    out = jax.ShapeDtypeStruct((m, c), BF)
    ospec = pl.BlockSpec((tr, c), lambda i: (i, 0))
    return pl.pallas_call(
        functools.partial(_short_kernel, seq=seq),
        grid=(m // tr,),
        in_specs=[pl.BlockSpec((tr, c3), cur), pl.BlockSpec((HALO, c3), prev),
                  pl.BlockSpec((HALO, c3), make_next(m // HALO)),
                  pl.BlockSpec((3, c3), lambda i: (0, 0)), pl.BlockSpec((1, c3), lambda i: (0, 0))],
        out_specs=[ospec, ospec, ospec],
        out_shape=[out, out, out],
        scratch_shapes=[pltpu.VMEM((tr + 2 * HALO, c3), F32)],
        compiler_params=_params("parallel"),
        name="hyena_short_conv",
    )(hy, hy, hy, w, b.reshape(1, c3))


def _cis(p, n, sign):
    ang = (2.0 * math.pi / n) * (p % n).astype(F32)
    return jnp.cos(ang), sign * jnp.sin(ang)


def _stack_complex(mr, mi):
    return jnp.concatenate([jnp.concatenate([mr, -mi], axis=-1),
                            jnp.concatenate([mi, mr], axis=-1)], axis=-2)


def _outer_stage_kernel(x_ref, t_ref, or_ref, oi_ref, yr_s, yi_s, *, nb):
    xt = pltpu.einshape("nrc->rnc", x_ref[...])
    for r in range(nb):
        y = jnp.dot(t_ref[r], xt[r], preferred_element_type=F32)
        h = y.shape[0] // 2
        yr_s[r] = y[:h].astype(BF)
        yi_s[r] = y[h:].astype(BF)
    or_ref[...] = pltpu.einshape("rnc->nrc", yr_s[...])
    oi_ref[...] = pltpu.einshape("rnc->nrc", yi_s[...])


def dft_outer_stage(x, table, nb, tc, name):
    bsz, rows, n_inner, c = x.shape
    _, two_k, _ = table.shape
    kout = two_k // 2
    tc = min(tc, c)
    out = jax.ShapeDtypeStruct((bsz, kout, n_inner, c), BF)
    ospec = pl.BlockSpec((None, kout, nb, tc), lambda b, j, cc: (b, 0, j, cc))
    return pl.pallas_call(
        functools.partial(_outer_stage_kernel, nb=nb),
        grid=(bsz, n_inner // nb, c // tc),
        in_specs=[pl.BlockSpec((None, rows, nb, tc), lambda b, j, cc: (b, 0, j, cc)),
                  pl.BlockSpec((nb, two_k, rows), lambda b, j, cc: (j, 0, 0))],
        out_specs=[ospec, ospec],
        out_shape=[out, out],
        scratch_shapes=[pltpu.VMEM((nb, kout, tc), BF), pltpu.VMEM((nb, kout, tc), BF)],
        compiler_params=_params("parallel", "arbitrary", "arbitrary"),
        name=name,
    )(x, table)


def _rowblock_stage_kernel(x_ref, t_ref, or_ref, oi_ref, *, nb, n1):
    for r in range(nb):
        rows = pl.ds(r * n1, n1)
        y = jnp.dot(t_ref[r], x_ref[rows, :], preferred_element_type=F32)
        h = y.shape[0] // 2
        or_ref[rows, :] = y[:h].astype(or_ref.dtype)
        oi_ref[rows, :] = y[h:].astype(oi_ref.dtype)


def filter_outer_stage(taps, table, n1, nb, tc):
    n, c = taps.shape
    tc = min(tc, c)
    spec = pl.BlockSpec((nb * n1, tc), lambda j, cc: (j, cc))
    out = jax.ShapeDtypeStruct((n, c), BF)
    return pl.pallas_call(
        functools.partial(_rowblock_stage_kernel, nb=nb, n1=n1),
        grid=(n // (nb * n1), c // tc),
        in_specs=[spec, pl.BlockSpec((nb, 2 * n1, n1), lambda j, cc: (j, 0, 0))],
        out_specs=[spec, spec],
        out_shape=[out, out],
        compiler_params=_params("parallel", "arbitrary"),
        name="filter_outer_dft",
    )(taps, table)


def _filter_inner_kernel(ur, ui, d, ss, kr_ref, ki_ref, *, kb, n2):
    scale = lax.rsqrt(ss[...] + EPS)
    urt = pltpu.einshape("nkc->knc", ur[...])
    uit = pltpu.einshape("nkc->knc", ui[...])
    for k in range(kb):
        rows = pl.ds(k * n2, n2)
        cat = jnp.concatenate([urt[k], uit[k]], axis=0)
        x = jnp.dot(d[...], cat, preferred_element_type=F32) * scale
        kr_ref[rows, :] = x[:n2].astype(kr_ref.dtype)
        ki_ref[rows, :] = x[n2:].astype(ki_ref.dtype)


def filter_inner_stage(ur, ui, d, ss, kb, tc):
    n2, n1, c = ur.shape
    tc = min(tc, c)
    ispec = pl.BlockSpec((n2, kb, tc), lambda i, cc: (0, i, cc))
    ospec = pl.BlockSpec((kb * n2, tc), lambda i, cc: (i, cc))
    out = jax.ShapeDtypeStruct((n1 * n2, c), BF)
    return pl.pallas_call(
        functools.partial(_filter_inner_kernel, kb=kb, n2=n2),
        grid=(n1 // kb, c // tc),
        in_specs=[ispec, ispec, pl.BlockSpec((2 * n2, 2 * n2), lambda i, cc: (0, 0)),
                  pl.BlockSpec((1, tc), lambda i, cc: (0, cc))],
        out_specs=[ospec, ospec],
        out_shape=[out, out],
        compiler_params=_params("parallel", "arbitrary"),
        name="filter_inner_dft",
    )(ur, ui, d, ss)


def _hyena_mid_kernel(ur, ui, kr, ki, d, minv, qr_ref, qi_ref, *, kb, n2):
    for k in range(kb):
        rows = pl.ds(k * n2, n2)
        cat = jnp.concatenate([ur[rows, :], ui[rows, :]], axis=0)
        x = jnp.dot(d[...], cat, preferred_element_type=F32)
        xr, xi = x[:n2], x[n2:]
        fr, fi = kr[rows, :].astype(F32), ki[rows, :].astype(F32)
        pr = xr * fr - xi * fi
        pi = xr * fi + xi * fr
        catp = jnp.concatenate([pr, pi], axis=0).astype(BF)
        q = jnp.dot(minv[k], catp, preferred_element_type=F32)
        qr_ref[rows, :] = q[:n2].astype(qr_ref.dtype)
        qi_ref[rows, :] = q[n2:].astype(qi_ref.dtype)


def hyena_mid_stage(ur, ui, kfr, kfi, d, minv, order, n2, kb):
    n, c = ur.shape
    n1 = n // n2
    kb = min(kb, n1)
    spec = pl.BlockSpec((kb * n2, c), lambda i: (i, 0))
    kspec = pl.BlockSpec((kb * n2, c), lambda i: (i, order))
    out = jax.ShapeDtypeStruct((n, c), BF)
    return pl.pallas_call(
        functools.partial(_hyena_mid_kernel, kb=kb, n2=n2),
        grid=(n1 // kb,),
        in_specs=[spec, spec, kspec, kspec,
                  pl.BlockSpec((2 * n2, 2 * n2), lambda i: (0, 0)),
                  pl.BlockSpec((kb, 2 * n2, 2 * n2), lambda i: (i, 0, 0))],
        out_specs=[spec, spec],
        out_shape=[out, out],
        compiler_params=_params("parallel"),
        name="hyena_mid",
    )(ur, ui, kfr, kfi, d, minv)


def _hyena_out_kernel(qr, qi, g, xo, z, skip, o_ref, y_s, *, nb):
    qrt = pltpu.einshape("nrc->rnc", qr[...])
    qit = pltpu.einshape("nrc->rnc", qi[...])
    for r in range(nb):
        cat = jnp.concatenate([qrt[r], qit[r]], axis=0)
        y_s[r] = jnp.dot(g[...], cat, preferred_element_type=F32)
    y = pltpu.einshape("rnc->nrc", y_s[...])
    zf = z[...].astype(F32)
    o_ref[...] = (xo[...].astype(F32) * (y + skip[...] * zf)).astype(o_ref.dtype)


def hyena_out_stage(qr, qi, g, xo, z, skip, nb, tc):
    rows, n2, c = qr.shape
    tc = min(tc, c)
    spec = pl.BlockSpec((rows, nb, tc), lambda j, cc: (0, j, cc))
    return pl.pallas_call(
        functools.partial(_hyena_out_kernel, nb=nb),
        grid=(n2 // nb, c // tc),
        in_specs=[spec, spec, pl.BlockSpec((rows, 2 * rows), lambda j, cc: (0, 0)), spec, spec,
                  pl.BlockSpec((1, tc), lambda j, cc: (0, cc))],
        out_specs=spec,
        out_shape=jax.ShapeDtypeStruct((rows, n2, c), BF),
        scratch_shapes=[pltpu.VMEM((nb, rows, tc), F32)],
        compiler_params=_params("parallel", "arbitrary"),
        name="hyena_out",
    )(qr, qi, g, xo, z, skip)


def _filter_kernel(ft, w1, b1, f1, w2, b2, f2, w3, decf, decb, k_ref, ss_ref):
    hi = lax.Precision.HIGHEST
    x = ft[...]
    ffn2 = w2.shape[1]
    h = jnp.sin(f1[...] * (jnp.dot(x, w1[...], precision=hi, preferred_element_type=F32) + b1[...]))
    h = jnp.sin(f2[...] * (jnp.dot(h, w2[...], precision=hi, preferred_element_type=F32) + b2[...]))
    t = x[:, 0:1]
    is_f = x[:, FWD_COL:FWD_COL + 1]
    is_b = x[:, BWD_COL:BWD_COL + 1]
    lane = lax.broadcasted_iota(jnp.int32, h.shape, 1)
    h = h * jnp.where(lane < ffn2 // 2, is_f, is_b)
    h = jnp.dot(h, w3[...], precision=hi, preferred_element_type=F32)
    dec = is_f * jnp.abs(decf[...]) + is_b * jnp.abs(decb[...])
    k = h * jnp.exp(-t * dec)
    k_ref[...] = k.astype(k_ref.dtype)
    part = jnp.sum(k * k, axis=0, keepdims=True)

    @pl.when(pl.program_id(0) == 0)
    def _():
        ss_ref[...] = jnp.zeros_like(ss_ref)

    ss_ref[...] += jnp.broadcast_to(part, ss_ref.shape)


def hyena_filter_taps(feats, w1, b1, f1, w2, b2, f2, w3, decf, decb, tr):
    n, fp = feats.shape
    ffn = w1.shape[1]
    ffn2 = w2.shape[1]
    oc = w3.shape[-1]
    small = lambda shape: pl.BlockSpec(shape, lambda i: (0,) * len(shape))
    taps, ss = pl.pallas_call(
        _filter_kernel,
        grid=(n // tr,),
        in_specs=[pl.BlockSpec((tr, fp), lambda i: (i, 0)),
                  small((fp, ffn)), small((1, ffn)), small((1, ffn)),
                  small((ffn, ffn2)), small((1, ffn2)), small((1, ffn2)),
                  small((ffn2, oc)), small((1, oc)), small((1, oc))],
        out_specs=[pl.BlockSpec((tr, oc), lambda i: (i, 0)),
                   pl.BlockSpec((8, oc), lambda i: (0, 0))],
        out_shape=[jax.ShapeDtypeStruct((n, oc), BF),
                   jax.ShapeDtypeStruct((8, oc), F32)],
        compiler_params=_params("arbitrary"),
        name="hyena_filter_taps",
    )(feats, w1, b1, f1, w2, b2, f2, w3, decf, decb)
    return taps, ss


def _fnet_kernel(yr, yi, d2, csg, w, b, o_ref, xcat, o_s, *, kb, n2, c):
    for k in range(kb):
        rows = pl.ds(k * n2, n2)
        cat = jnp.concatenate([yr[rows, :], yi[rows, :]], axis=0)
        x = jnp.dot(d2[...], cat, preferred_element_type=F32)
        xcat[rows, pl.ds(0, c)] = x[:n2].astype(BF)
        xcat[rows, pl.ds(c, c)] = x[n2:].astype(BF)
    ff = jnp.dot(xcat[...], csg[...], preferred_element_type=F32)
    out = jnp.dot(ff.astype(BF), w[...], preferred_element_type=F32) + b[...]
    for k in range(kb):
        o_s[k] = out[k * n2:(k + 1) * n2].astype(o_s.dtype)
    o_ref[...] = pltpu.einshape("kmc->mkc", o_s[...])


def fnet_inner_stage(yr, yi, d2, csg, w, b, n2, kb):
    bsz, l, c = yr.shape
    n1 = l // n2
    kb = min(kb, n1)
    spec = pl.BlockSpec((None, kb * n2, c), lambda bb, i: (bb, i, 0))
    const = lambda shape: pl.BlockSpec(shape, lambda bb, i: (0,) * len(shape))
    return pl.pallas_call(
        functools.partial(_fnet_kernel, kb=kb, n2=n2, c=c),
        grid=(bsz, n1 // kb),
        in_specs=[spec, spec, const((2 * n2, 2 * n2)), const((2 * c, c)), const((c, c)), const((1, c))],
        out_specs=pl.BlockSpec((None, n2, kb, c), lambda bb, i: (bb, 0, i, 0)),
        out_shape=jax.ShapeDtypeStruct((bsz, n2, n1, c), BF),
        scratch_shapes=[pltpu.VMEM((kb * n2, 2 * c), BF), pltpu.VMEM((kb, n2, c), BF)],
        compiler_params=_params("parallel", "arbitrary"),
        name="fnet_inner",
    )(yr, yi, d2, csg, w, b)


def _attn_kernel(q_ref, k_ref, v_ref, o_ref, *, heads):
    c = q_ref.shape[1]
    hd = c // heads
    scale = hd ** -0.5
    for h in range(heads):
        cols = pl.ds(h * hd, hd)
        s = lax.dot_general(q_ref[:, cols], k_ref[:, cols], (((1,), (1,)), ((), ())),
                            preferred_element_type=F32) * scale
        s = s - jnp.max(s, axis=-1, keepdims=True)
        e = jnp.exp(s)
        p = e / jnp.sum(e, axis=-1, keepdims=True)
        o = jnp.dot(p.astype(BF), v_ref[:, cols], preferred_element_type=F32)
        o_ref[:, cols] = o.astype(o_ref.dtype)


def memory_attention(q, kh, vh, seq, tr):
    m, c = q.shape
    _, nm, _ = kh.shape
    per = seq // tr
    kv = pl.BlockSpec((None, nm, c), lambda i: (i // per, 0, 0))
    return pl.pallas_call(
        functools.partial(_attn_kernel, heads=MEM_HEADS),
        grid=(m // tr,),
        in_specs=[pl.BlockSpec((tr, c), lambda i: (i, 0)), kv, kv],
        out_specs=pl.BlockSpec((tr, c), lambda i: (i, 0)),
        out_shape=jax.ShapeDtypeStruct((m, c), BF),
        compiler_params=_params("parallel"),
        name="memory_attention",
    )(q, kh, vh)


def _merge_kernel(ya, yb, yc, ym, gate, gg, w, x, pg, o_ref):
    g = ya.shape[1]
    acc = None
    for idx, y_ref in enumerate((ya, yb, yc, ym)):
        cols = pl.ds(idx * g, g)
        y = y_ref[...].astype(F32)
        ms = jnp.mean(y * y, axis=-1, keepdims=True)
        yn = y * lax.rsqrt(ms + EPS) * gg[:, cols]
        gt = gate[:, cols].astype(F32)
        yn = yn * (gt * jax.nn.sigmoid(gt))
        part = jnp.dot(yn.astype(BF), w[cols, :], preferred_element_type=F32)
        acc = part if acc is None else acc + part
    ms = jnp.mean(acc * acc, axis=-1, keepdims=True)
    o_ref[...] = x[...] + acc * lax.rsqrt(ms + EPS) * pg[...]


def merge_project(ya, yb, yc, ym, gate, gg, w_out, x, pg, tm):
    m, g = ya.shape
    d = x.shape[1]
    yspec = pl.BlockSpec((tm, g), lambda i: (i, 0))
    const = lambda shape: pl.BlockSpec(shape, lambda i: (0, 0))
    return pl.pallas_call(
        _merge_kernel,
        grid=(m // tm,),
        in_specs=[yspec, yspec, yspec, yspec,
                  pl.BlockSpec((tm, 4 * g), lambda i: (i, 0)),
                  const((1, 4 * g)),
                  pl.BlockSpec((4 * g, d), lambda i: (0, 0), pipeline_mode=pl.Buffered(1)),
                  pl.BlockSpec((tm, d), lambda i: (i, 0)),
                  const((1, d))],
        out_specs=pl.BlockSpec((tm, d), lambda i: (i, 0)),
        out_shape=jax.ShapeDtypeStruct((m, d), F32),
        compiler_params=_params("parallel"),
        name="merge_project",
    )(ya, yb, yc, ym, gate, gg, w_out, x, pg)


def _iota(n):
    return jnp.arange(n, dtype=jnp.int32)


def _outer_tables(n1, n2, rows_complex):
    n = n1 * n2
    r = _iota(n2)[:, None, None]
    k1 = _iota(n1)[None, :, None]
    nj = n1 // 2 if rows_complex else n1
    j = _iota(nj)[None, None, :]
    er, ei = _cis(k1 * (n2 * j + r), n, -1.0)
    if rows_complex:
        return _stack_complex(er, ei).astype(BF)
    return jnp.concatenate([er, ei], axis=-2).astype(BF)


def _inner_table(n2):
    k = _iota(n2)[:, None]
    j = _iota(n2)[None, :]
    cr, ci = _cis(k * j, n2, -1.0)
    return _stack_complex(cr, ci).astype(BF)


def _inverse_inner_tables(n1, n2):
    n = n1 * n2
    k1 = _iota(n1)[:, None, None]
    na = _iota(n2)[None, :, None]
    k2 = _iota(n2)[None, None, :]
    ar, ai = _cis(na * (k1 + n1 * k2), n, 1.0)
    return _stack_complex(ar, ai).astype(BF)


def _inverse_outer_table(n1, n2):
    nb = _iota(n1 // 2)[:, None]
    k1 = _iota(n1)[None, :]
    gr, gi = _cis(nb * k1, n1, 1.0)
    inv = 1.0 / (n1 * n2)
    return _stack_complex(gr * inv, gi * inv).astype(BF)


def _channel_table(g, scale):
    a = _iota(g)[:, None]
    b = _iota(g)[None, :]
    cg, sg = _cis(a * b, g, 1.0)
    return (jnp.concatenate([cg, sg], axis=0) * scale).astype(BF)


def _filter_features(positions, seq, n1, n2):
    idx = (n2 * _iota(n1)[None, :] + _iota(n2)[:, None]).reshape(-1)
    pos = positions.astype(F32)
    pos = jnp.concatenate([pos, pos[:1], jnp.flip(pos[1:])])
    pos = pos.reshape(n1, n2).T.reshape(-1)
    t = pos / seq
    bands = jnp.linspace(1e-4, HYENA_BANDS - 1, HYENA_BANDS, dtype=F32)
    w = 2.0 * jnp.pi * pos / seq
    is_f = (idx < seq).astype(F32)
    is_b = (idx > seq).astype(F32)
    feats = jnp.concatenate([t[:, None], jnp.cos(w[:, None] * bands), jnp.sin(w[:, None] * bands),
                             is_f[:, None], is_b[:, None]], axis=-1)
    return jnp.pad(feats, ((0, 0), (0, FEAT_PAD - feats.shape[1])))


def _layer(x2, mem_n, feats, tabs, p, dims):
    bsz, seq, d, g = dims
    m = bsz * seq
    n2 = FFT_INNER
    n1 = 2 * seq // n2
    n1f, n2f = FFT_INNER, seq // FFT_INNER

    h = rmsnorm_bf16(x2, p["pre_g"], 512)
    w_in = p["w_in"].astype(BF)
    tm, tn = 1024, g
    a = matmul(h, w_in, tm, tn, "proj_a", 0, 2 * g)
    f = matmul(h, w_in, tm, tn, "proj_f", 2 * g, g)
    hy = matmul(h, w_in, tm, tn, "proj_hy", 3 * g, 3 * g)
    q = matmul(h, w_in, tm, tn, "proj_q", 6 * g, g)
    gate = matmul(h, w_in, tm, tn, "proj_gate", 7 * g, 4 * g)

    y_a = conv_module(a, p["conv_dw_w"], p["conv_dw_b"], p["conv_ln_g"], p["conv_ln_b"],
                      p["conv_pw_w"], p["conv_pw_b"], seq, 512)

    yr, yi = dft_outer_stage(f.reshape(bsz, n1f, n2f, g), tabs["fnet_outer"], DFT_RESIDUES, 512, "fnet_outer")
    y_b = fnet_inner_stage(yr.reshape(bsz, seq, g), yi.reshape(bsz, seq, g), tabs["fnet_inner"],
                           tabs["fnet_chan"], p["fnet_w"].astype(BF), p["fnet_b"].reshape(1, g), n2f, 16)
    y_b = y_b.reshape(m, g)

    v, x1, x2g = hyena_short_conv(hy, p["hy_short_w"], p["hy_short_b"], seq, 512)
    ffn = p["hy_fw1"].shape[1]
    og = HYENA_ORDER * g
    w1 = jnp.pad(p["hy_fw1"], ((0, FEAT_PAD - p["hy_fw1"].shape[0]), (0, 0)))
    dup = lambda a: jnp.concatenate([a, a], axis=-1)
    w3 = p["hy_fw3"].reshape(ffn, HYENA_ORDER, 2, g)
    w3 = jnp.concatenate([w3[:, :, 0, :].reshape(ffn, og), w3[:, :, 1, :].reshape(ffn, og)], axis=0)
    taps, ss = hyena_filter_taps(feats, w1, p["hy_fb1"].reshape(1, ffn), p["hy_freq1"].reshape(1, ffn),
                                 dup(p["hy_fw2"]), dup(p["hy_fb2"].reshape(1, ffn)), dup(p["hy_freq2"].reshape(1, ffn)),
                                 w3, p["hy_decay"][:, 0, :].reshape(1, og), p["hy_decay"][:, 1, :].reshape(1, og), 512)
    kur, kui = filter_outer_stage(taps, tabs["filt_outer"], n1, 4, 1024)
    kfr, kfi = filter_inner_stage(kur.reshape(n2, n1, og), kui.reshape(n2, n1, og), tabs["inner"], ss[0:1],
                                  DFT_RESIDUES, 1024)

    z = v
    for o, gate_o in enumerate((x1, x2g)):
        ur, ui = dft_outer_stage(z.reshape(1, n1, n2, g), tabs["hy_outer"], DFT_RESIDUES, 512, "hyena_outer_dft")
        qr, qi = hyena_mid_stage(ur.reshape(n1 * n2, g), ui.reshape(n1 * n2, g), kfr, kfi,
                                 tabs["inner"], tabs["inv_inner"], o, n2, 8)
        z = hyena_out_stage(qr.reshape(n1, n2, g), qi.reshape(n1, n2, g), tabs["inv_outer"],
                            gate_o.reshape(n1, n2, g), z.reshape(n1, n2, g), p["hy_skip"][o].reshape(1, g),
                            DFT_RESIDUES, 512)
    y_c = z.reshape(m, g)

    nm = mem_n.shape[0] // bsz
    kh = matmul(mem_n, p["mem_wk"].astype(BF), 512, 1024, "mem_k").reshape(bsz, nm, g)
    vh = matmul(mem_n, p["mem_wv"].astype(BF), 512, 1024, "mem_v").reshape(bsz, nm, g)
    y_m = memory_attention(q, kh, vh, seq, 1024)

    return merge_project(y_a, y_b, y_c, y_m, gate, p["group_g"].reshape(1, 4 * g), p["w_out"].astype(BF),
                         x2, p["post_g"].reshape(1, d), 256)


def kernel(x, mem, positions, mem_norm_g, pre_norm_g, post_norm_g, w_in, conv_dw_w, conv_dw_b, conv_ln_g, conv_ln_b, conv_pw_w, conv_pw_b, fnet_w, fnet_b, hy_short_w, hy_short_b, hy_fw1, hy_fb1, hy_freq1, hy_fw2, hy_fb2, hy_freq2, hy_fw3, hy_decay, hy_skip, mem_wk, mem_wv, group_norm_g, w_out):
    bsz, seq, d = x.shape
    g = conv_pw_w.shape[-1]
    depth = w_in.shape[0]
    assert bsz == 2, "the long convolution packs exactly two batch elements as one complex sequence"
    assert seq % (FFT_INNER * DFT_RESIDUES) == 0
    n2 = FFT_INNER
    n1 = 2 * seq // n2
    n1f, n2f = FFT_INNER, seq // FFT_INNER

    tabs = {
        "fnet_outer": _outer_tables(n1f, n2f, False),
        "fnet_inner": _inner_table(n2f),
        "fnet_chan": _channel_table(g, 1.0 / math.sqrt(seq * g)),
        "hy_outer": _outer_tables(n1, n2, True),
        "filt_outer": _outer_tables(n1, n2, False),
        "inner": _inner_table(n2),
        "inv_inner": _inverse_inner_tables(n1, n2),
        "inv_outer": _inverse_outer_table(n1, n2),
    }
    feats = _filter_features(positions, seq, n1, n2)
    mem_n = rmsnorm_bf16(mem.reshape(-1, d), mem_norm_g, min(512, mem.shape[0] * mem.shape[1]))

    x2 = x.reshape(bsz * seq, d)
    for l in range(depth):
        p = {
            "pre_g": pre_norm_g[l], "post_g": post_norm_g[l], "w_in": w_in[l],
            "conv_dw_w": conv_dw_w[l], "conv_dw_b": conv_dw_b[l], "conv_ln_g": conv_ln_g[l],
            "conv_ln_b": conv_ln_b[l], "conv_pw_w": conv_pw_w[l], "conv_pw_b": conv_pw_b[l],
            "fnet_w": fnet_w[l], "fnet_b": fnet_b[l], "hy_short_w": hy_short_w[l], "hy_short_b": hy_short_b[l],
            "hy_fw1": hy_fw1[l], "hy_fb1": hy_fb1[l], "hy_freq1": hy_freq1[l], "hy_fw2": hy_fw2[l],
            "hy_fb2": hy_fb2[l], "hy_freq2": hy_freq2[l], "hy_fw3": hy_fw3[l], "hy_decay": hy_decay[l],
            "hy_skip": hy_skip[l], "mem_wk": mem_wk[l], "mem_wv": mem_wv[l], "group_g": group_norm_g[l],
            "w_out": w_out[l],
        }
        x2 = _layer(x2, mem_n, feats, tabs, p, (bsz, seq, d, g))
    return x2.reshape(bsz, seq, d)
```

```python
import functools
import math

import jax
import jax.numpy as jnp
import numpy as np
from jax import lax
from jax.experimental import pallas as pl
from jax.experimental.pallas import tpu as pltpu

BF = jnp.bfloat16
F32 = jnp.float32
EPS = 1e-6

CONV_WIDTH = 31
CONV_HALF = CONV_WIDTH // 2
HALO = 16
SUBLANES, LANES = 8, 128
HYENA_BANDS = 16
HYENA_ORDER = 2
MEM_HEADS = 4
FFT_INNER = 128
FEAT_PAD = 128
FWD_COL = 2 * HYENA_BANDS + 1
BWD_COL = 2 * HYENA_BANDS + 2
DFT_RESIDUES = 16
VMEM_LIMIT = 56 * 1024 * 1024


def _params(*sem):
    return pltpu.CompilerParams(dimension_semantics=sem, vmem_limit_bytes=VMEM_LIMIT)


def _rmsnorm_kernel(x_ref, g_ref, o_ref):
    x = x_ref[...]
    ms = jnp.mean(x * x, axis=-1, keepdims=True)
    o_ref[...] = (x * lax.rsqrt(ms + EPS) * g_ref[...]).astype(o_ref.dtype)


def rmsnorm_bf16(x, g, tm):
    m, d = x.shape
    return pl.pallas_call(
        _rmsnorm_kernel,
        grid=(m // tm,),
        in_specs=[pl.BlockSpec((tm, d), lambda i: (i, 0)),
                  pl.BlockSpec((1, d), lambda i: (0, 0))],
        out_specs=pl.BlockSpec((tm, d), lambda i: (i, 0)),
        out_shape=jax.ShapeDtypeStruct((m, d), BF),
        compiler_params=_params("parallel"),
        name="rmsnorm",
    )(x, g.reshape(1, d))


def _mm_kernel(a_ref, w_ref, o_ref):
    o_ref[...] = jnp.dot(a_ref[...], w_ref[...], preferred_element_type=F32).astype(o_ref.dtype)


def matmul(a, w, tm, tn, name, col0=0, n=None):
    m, k = a.shape
    n = w.shape[1] if n is None else n
    tm, tn = min(tm, m), min(tn, n)
    joff = col0 // tn
    return pl.pallas_call(
        _mm_kernel,
        grid=(m // tm, n // tn),
        in_specs=[pl.BlockSpec((tm, k), lambda i, j: (i, 0)),
                  pl.BlockSpec((k, tn), lambda i, j: (0, j + joff))],
        out_specs=pl.BlockSpec((tm, tn), lambda i, j: (i, j)),
        out_shape=jax.ShapeDtypeStruct((m, n), BF),
        compiler_params=_params("parallel", "arbitrary"),
        name=name,
    )(a, w)


def _proj_kernel(x_ref, g_ref, w_ref, o_ref, h_s):
    @pl.when(pl.program_id(1) == 0)
    def _():
        x = x_ref[...]
        ms = jnp.mean(x * x, axis=-1, keepdims=True)
        h_s[...] = (x * lax.rsqrt(ms + EPS) * g_ref[...]).astype(h_s.dtype)

    o_ref[...] = jnp.dot(h_s[...], w_ref[...], preferred_element_type=F32).astype(o_ref.dtype)


def input_projection(x, g, w, tm, tn):
    m, d = x.shape
    n = w.shape[1]
    return pl.pallas_call(
        _proj_kernel,
        grid=(m // tm, n // tn),
        in_specs=[pl.BlockSpec((tm, d), lambda i, j: (i, 0)),
                  pl.BlockSpec((1, d), lambda i, j: (0, 0)),
                  pl.BlockSpec((d, tn), lambda i, j: (0, j))],
        out_specs=pl.BlockSpec((tm, tn), lambda i, j: (i, j)),
        out_shape=jax.ShapeDtypeStruct((m, n), BF),
        scratch_shapes=[pltpu.VMEM((tm, d), BF)],
        compiler_params=_params("parallel", "arbitrary"),
        name="input_projection",
    )(x, g.reshape(1, d), w)


def _halo_specs(tr, width, col):
    per = tr // HALO

    def cur(i):
        return (i, col)

    def prev(i):
        return (jnp.maximum(i * per - 1, 0), col)

    def make_next(nblocks):
        def nxt(i):
            return (jnp.minimum((i + 1) * per, nblocks - 1), col)
        return nxt

    return cur, prev, make_next


def _seq_edges(i, tr, seq):
    tiles = seq // tr
    pos = i % tiles
    return pos == 0, pos == tiles - 1


def _conva_kernel(vc, gc, vp, gp, vn, gn, dww, dwb, lng, lnb, pww, pwb, o_ref, upad, cbuf, wbc, *, seq):
    tr, c = vc.shape
    first, last = _seq_edges(pl.program_id(0), tr, seq)

    def glu(v, g):
        return v[...].astype(F32) * jax.nn.sigmoid(g[...].astype(F32))

    upad[pl.ds(HALO, tr), :] = glu(vc, gc)
    upad[pl.ds(0, HALO), :] = jnp.where(first, 0.0, glu(vp, gp))
    upad[pl.ds(HALO + tr, HALO), :] = jnp.where(last, 0.0, glu(vn, gn))

    for j in range(CONV_WIDTH):
        wbc[pl.ds(SUBLANES * j, SUBLANES), :] = jnp.broadcast_to(dww[pl.ds(j, 1), :], (SUBLANES, c))
    row = lax.broadcasted_iota(jnp.int32, (SUBLANES, LANES), 0)
    for ct in range(c // LANES):
        lanes = pl.ds(ct * LANES, LANES)

        def residue_sums(b, lanes=lanes):
            u = [upad[pl.ds(pl.multiple_of((b + a) * SUBLANES, SUBLANES), SUBLANES), lanes] for a in range(4)]
            sums = []
            for s in range(SUBLANES):
                acc = None
                for a in range(4):
                    j = SUBLANES * a + s - 1
                    if 0 <= j < CONV_WIDTH:
                        term = u[a] * wbc[pl.ds(SUBLANES * j, SUBLANES), lanes]
                        acc = term if acc is None else acc + term
                sums.append(acc)
            return tuple(sums)

        bias = jnp.broadcast_to(dwb[:, lanes], (SUBLANES, LANES))

        def body(b, prev, lanes=lanes, bias=bias, residue_sums=residue_sums):
            cur = residue_sums(b)
            terms = [bias + prev[0]]
            for s in range(1, SUBLANES):
                terms.append(pltpu.roll(jnp.where(row >= s, prev[s], cur[s]), SUBLANES - s, axis=0))
            while len(terms) > 1:
                terms = [terms[i] + terms[i + 1] for i in range(0, len(terms), 2)]
            out = terms[0]
            cbuf[pl.ds(pl.multiple_of((b - 1) * SUBLANES, SUBLANES), SUBLANES), lanes] = out
            return cur

        lax.fori_loop(1, tr // SUBLANES + 1, body, residue_sums(0), unroll=4)

    u = cbuf[...]
    mu = jnp.mean(u, axis=-1, keepdims=True)
    d = u - mu
    var = jnp.mean(d * d, axis=-1, keepdims=True)
    y = d * lax.rsqrt(var + EPS) * lng[...] + lnb[...]
    y = y * jax.nn.sigmoid(y)
    out = jnp.dot(y.astype(BF), pww[...], preferred_element_type=F32) + pwb[...]
    o_ref[...] = out.astype(o_ref.dtype)


def conv_module(a, dww, dwb, lng, lnb, pww, pwb, seq, tr):
    m = a.shape[0]
    c = pww.shape[0]
    cur, prev, make_next = _halo_specs(tr, c, 0)
    cur_g, prev_g, make_next_g = _halo_specs(tr, c, 1)
    nh = m // HALO
    row = lambda i: (0, 0)
    return pl.pallas_call(
        functools.partial(_conva_kernel, seq=seq),
        grid=(m // tr,),
        in_specs=[pl.BlockSpec((tr, c), cur), pl.BlockSpec((tr, c), cur_g),
                  pl.BlockSpec((HALO, c), prev), pl.BlockSpec((HALO, c), prev_g),
                  pl.BlockSpec((HALO, c), make_next(nh)), pl.BlockSpec((HALO, c), make_next_g(nh)),
                  pl.BlockSpec((CONV_WIDTH, c), row), pl.BlockSpec((1, c), row),
                  pl.BlockSpec((1, c), row), pl.BlockSpec((1, c), row),
                  pl.BlockSpec((c, c), row), pl.BlockSpec((1, c), row)],
        out_specs=pl.BlockSpec((tr, c), lambda i: (i, 0)),
        out_shape=jax.ShapeDtypeStruct((m, c), BF),
        scratch_shapes=[pltpu.VMEM((tr + 2 * HALO, c), F32), pltpu.VMEM((tr, c), F32),
                        pltpu.VMEM((CONV_WIDTH * SUBLANES, c), F32)],
        compiler_params=_params("parallel"),
        name="conv_module",
    )(a, a, a, a, a, a, dww, dwb.reshape(1, c), lng.reshape(1, c), lnb.reshape(1, c),
      pww.astype(BF), pwb.reshape(1, c))


def _short_kernel(hc, hp, hn, w, b, v_ref, x1_ref, x2_ref, upad, *, seq):
    tr, c3 = hc.shape
    c = c3 // 3
    first, last = _seq_edges(pl.program_id(0), tr, seq)
    upad[pl.ds(HALO, tr), :] = hc[...].astype(F32)
    upad[pl.ds(0, HALO), :] = jnp.where(first, 0.0, hp[...].astype(F32))
    upad[pl.ds(HALO + tr, HALO), :] = jnp.where(last, 0.0, hn[...].astype(F32))
    for g, o_ref in enumerate((v_ref, x1_ref, x2_ref)):
        cols = pl.ds(g * c, c)
        y = (upad[pl.ds(HALO - 1, tr), cols] * w[pl.ds(0, 1), cols]
             + upad[pl.ds(HALO, tr), cols] * w[pl.ds(1, 1), cols]
             + upad[pl.ds(HALO + 1, tr), cols] * w[pl.ds(2, 1), cols] + b[:, cols])
        o_ref[...] = y.astype(o_ref.dtype)


def hyena_short_conv(hy, w, b, seq, tr, col):
    m = hy.shape[0]
    c3 = w.shape[1]
    c = c3 // 3
    cur, prev, make_next = _halo_specs(tr, c3, col)
    out = jax.ShapeDtypeStruct((m, c), BF)
    ospec = pl.BlockSpec((tr, c), lambda i: (i, 0))
    return pl.pallas_call(
        functools.partial(_short_kernel, seq=seq),
        grid=(m // tr,),
        in_specs=[pl.BlockSpec((tr, c3), cur), pl.BlockSpec((HALO, c3), prev),
                  pl.BlockSpec((HALO, c3), make_next(m // HALO)),
                  pl.BlockSpec((3, c3), lambda i: (0, 0)), pl.BlockSpec((1, c3), lambda i: (0, 0))],
        out_specs=[ospec, ospec, ospec],
        out_shape=[out, out, out],
        scratch_shapes=[pltpu.VMEM((tr + 2 * HALO, c3), F32)],
        compiler_params=_params("parallel"),
        name="hyena_short_conv",
    )(hy, hy, hy, w, b.reshape(1, c3))


def _cis(p, n, sign):
    ang = (2.0 * np.pi / n) * (p % n)
    return np.cos(ang), sign * np.sin(ang)


def _stack_complex(mr, mi):
    return np.concatenate([np.concatenate([mr, -mi], axis=-1),
                           np.concatenate([mi, mr], axis=-1)], axis=-2)


def _bf16_const(a):
    return a.astype(jnp.bfloat16)


def _outer_stage_kernel(x_ref, t_ref, or_ref, oi_ref, yr_s, yi_s, *, nb):
    xt = pltpu.einshape("nrc->rnc", x_ref[...])
    for r in range(nb):
        y = jnp.dot(t_ref[r], xt[r], preferred_element_type=F32)
        h = y.shape[0] // 2
        yr_s[r] = y[:h].astype(BF)
        yi_s[r] = y[h:].astype(BF)
    or_ref[...] = pltpu.einshape("rnc->nrc", yr_s[...])
    oi_ref[...] = pltpu.einshape("rnc->nrc", yi_s[...])


def dft_outer_stage(x, table, nb, tc, name, c=None, col0=0):
    bsz, rows, n_inner, _ = x.shape
    c = x.shape[3] if c is None else c
    _, two_k, _ = table.shape
    kout = two_k // 2
    tc = min(tc, c)
    coff = col0 // tc
    out = jax.ShapeDtypeStruct((bsz, kout, n_inner, c), BF)
    ospec = pl.BlockSpec((None, kout, nb, tc), lambda b, j, cc: (b, 0, j, cc))
    return pl.pallas_call(
        functools.partial(_outer_stage_kernel, nb=nb),
        grid=(bsz, n_inner // nb, c // tc),
        in_specs=[pl.BlockSpec((None, rows, nb, tc), lambda b, j, cc: (b, 0, j, cc + coff)),
                  pl.BlockSpec((nb, two_k, rows), lambda b, j, cc: (j, 0, 0))],
        out_specs=[ospec, ospec],
        out_shape=[out, out],
        scratch_shapes=[pltpu.VMEM((nb, kout, tc), BF), pltpu.VMEM((nb, kout, tc), BF)],
        compiler_params=_params("parallel", "arbitrary", "arbitrary"),
        name=name,
    )(x, table)


def _rowblock_stage_kernel(x_ref, t_ref, or_ref, oi_ref, *, nb, n1):
    for r in range(nb):
        rows = pl.ds(r * n1, n1)
        y = jnp.dot(t_ref[r], x_ref[rows, :], preferred_element_type=F32)
        h = y.shape[0] // 2
        or_ref[rows, :] = y[:h].astype(or_ref.dtype)
        oi_ref[rows, :] = y[h:].astype(oi_ref.dtype)


def filter_outer_stage(taps, table, n1, nb, tc):
    n, c = taps.shape
    tc = min(tc, c)
    spec = pl.BlockSpec((nb * n1, tc), lambda j, cc: (j, cc))
    out = jax.ShapeDtypeStruct((n, c), BF)
    return pl.pallas_call(
        functools.partial(_rowblock_stage_kernel, nb=nb, n1=n1),
        grid=(n // (nb * n1), c // tc),
        in_specs=[spec, pl.BlockSpec((nb, 2 * n1, n1), lambda j, cc: (j, 0, 0))],
        out_specs=[spec, spec],
        out_shape=[out, out],
        compiler_params=_params("parallel", "arbitrary"),
        name="filter_outer_dft",
    )(taps, table)


def _filter_inner_kernel(ur, ui, d, ss, kr_ref, ki_ref, *, kb, n2):
    scale = lax.rsqrt(ss[...] + EPS)
    urt = pltpu.einshape("nkc->knc", ur[...])
    uit = pltpu.einshape("nkc->knc", ui[...])
    for k in range(kb):
        rows = pl.ds(k * n2, n2)
        cat = jnp.concatenate([urt[k], uit[k]], axis=0)
        x = jnp.dot(d[...], cat, preferred_element_type=F32) * scale
        kr_ref[rows, :] = x[:n2].astype(kr_ref.dtype)
        ki_ref[rows, :] = x[n2:].astype(ki_ref.dtype)


def filter_inner_stage(ur, ui, d, ss, kb, tc):
    n2, n1, c = ur.shape
    tc = min(tc, c)
    ispec = pl.BlockSpec((n2, kb, tc), lambda i, cc: (0, i, cc))
    ospec = pl.BlockSpec((kb * n2, tc), lambda i, cc: (i, cc))
    out = jax.ShapeDtypeStruct((n1 * n2, c), BF)
    return pl.pallas_call(
        functools.partial(_filter_inner_kernel, kb=kb, n2=n2),
        grid=(n1 // kb, c // tc),
        in_specs=[ispec, ispec, pl.BlockSpec((2 * n2, 2 * n2), lambda i, cc: (0, 0)),
                  pl.BlockSpec((1, tc), lambda i, cc: (0, cc))],
        out_specs=[ospec, ospec],
        out_shape=[out, out],
        compiler_params=_params("parallel", "arbitrary"),
        name="filter_inner_dft",
    )(ur, ui, d, ss)


def _hyena_mid_kernel(ur, ui, kr, ki, d, minv, qr_ref, qi_ref, *, kb, n2):
    for k in range(kb):
        rows = pl.ds(k * n2, n2)
        cat = jnp.concatenate([ur[rows, :], ui[rows, :]], axis=0)
        x = jnp.dot(d[...], cat, preferred_element_type=F32)
        xr, xi = x[:n2], x[n2:]
        fr, fi = kr[rows, :].astype(F32), ki[rows, :].astype(F32)
        pr = xr * fr - xi * fi
        pi = xr * fi + xi * fr
        catp = jnp.concatenate([pr, pi], axis=0).astype(BF)
        q = jnp.dot(minv[k], catp, preferred_element_type=F32)
        qr_ref[rows, :] = q[:n2].astype(qr_ref.dtype)
        qi_ref[rows, :] = q[n2:].astype(qi_ref.dtype)


def hyena_mid_stage(ur, ui, kfr, kfi, d, minv, order, n2, kb):
    n, c = ur.shape
    n1 = n // n2
    kb = min(kb, n1)
    spec = pl.BlockSpec((kb * n2, c), lambda i: (i, 0))
    kspec = pl.BlockSpec((kb * n2, c), lambda i: (i, order))
    out = jax.ShapeDtypeStruct((n, c), BF)
    return pl.pallas_call(
        functools.partial(_hyena_mid_kernel, kb=kb, n2=n2),
        grid=(n1 // kb,),
        in_specs=[spec, spec, kspec, kspec,
                  pl.BlockSpec((2 * n2, 2 * n2), lambda i: (0, 0)),
                  pl.BlockSpec((kb, 2 * n2, 2 * n2), lambda i: (i, 0, 0))],
        out_specs=[spec, spec],
        out_shape=[out, out],
        compiler_params=_params("parallel"),
        name="hyena_mid",
    )(ur, ui, kfr, kfi, d, minv)


def _hyena_out_kernel(qr, qi, g, xo, z, skip, o_ref, y_s, *, nb):
    qrt = pltpu.einshape("nrc->rnc", qr[...])
    qit = pltpu.einshape("nrc->rnc", qi[...])
    for r in range(nb):
        cat = jnp.concatenate([qrt[r], qit[r]], axis=0)
        y_s[r] = jnp.dot(g[...], cat, preferred_element_type=F32)
    y = pltpu.einshape("rnc->nrc", y_s[...])
    zf = z[...].astype(F32)
    o_ref[...] = (xo[...].astype(F32) * (y + skip[...] * zf)).astype(o_ref.dtype)


def hyena_out_stage(qr, qi, g, xo, z, skip, nb, tc):
    rows, n2, c = qr.shape
    tc = min(tc, c)
    spec = pl.BlockSpec((rows, nb, tc), lambda j, cc: (0, j, cc))
    return pl.pallas_call(
        functools.partial(_hyena_out_kernel, nb=nb),
        grid=(n2 // nb, c // tc),
        in_specs=[spec, spec, pl.BlockSpec((rows, 2 * rows), lambda j, cc: (0, 0)), spec, spec,
                  pl.BlockSpec((1, tc), lambda j, cc: (0, cc))],
        out_specs=spec,
        out_shape=jax.ShapeDtypeStruct((rows, n2, c), BF),
        scratch_shapes=[pltpu.VMEM((nb, rows, tc), F32)],
        compiler_params=_params("parallel", "arbitrary"),
        name="hyena_out",
    )(qr, qi, g, xo, z, skip)


def _dot_bf16x3(a, b):
    ah = a.astype(BF)
    bh = b.astype(BF)
    al = (a - ah.astype(F32)).astype(BF)
    bl = (b - bh.astype(F32)).astype(BF)
    dot = functools.partial(jnp.dot, preferred_element_type=F32)
    return dot(ah, bh) + dot(al, bh) + dot(ah, bl)


def _filter_kernel(ft, w1, b1, f1, w2, b2, f2, w3, decf, decb, k_ref, ss_ref):
    hi = lax.Precision.HIGHEST
    x = ft[...]
    ffn2 = w2.shape[1]
    h = jnp.sin(f1[...] * (jnp.dot(x, w1[...], precision=hi, preferred_element_type=F32) + b1[...]))
    h = jnp.sin(f2[...] * (jnp.dot(h, w2[...], precision=hi, preferred_element_type=F32) + b2[...]))
    t = x[:, 0:1]
    is_f = x[:, FWD_COL:FWD_COL + 1]
    is_b = x[:, BWD_COL:BWD_COL + 1]
    lane = lax.broadcasted_iota(jnp.int32, h.shape, 1)
    h = h * jnp.where(lane < ffn2 // 2, is_f, is_b)
    h = _dot_bf16x3(h, w3[...])
    dec = is_f * jnp.abs(decf[...]) + is_b * jnp.abs(decb[...])
    k = h * jnp.exp(-t * dec)
    k_ref[...] = k.astype(k_ref.dtype)
    part = jnp.sum(k * k, axis=0, keepdims=True)

    @pl.when(pl.program_id(0) == 0)
    def _():
        ss_ref[...] = jnp.zeros_like(ss_ref)

    ss_ref[...] += jnp.broadcast_to(part, ss_ref.shape)


def hyena_filter_taps(feats, w1, b1, f1, w2, b2, f2, w3, decf, decb, tr):
    n, fp = feats.shape
    ffn = w1.shape[1]
    ffn2 = w2.shape[1]
    oc = w3.shape[-1]
    small = lambda shape: pl.BlockSpec(shape, lambda i: (0,) * len(shape))
    taps, ss = pl.pallas_call(
        _filter_kernel,
        grid=(n // tr,),
        in_specs=[pl.BlockSpec((tr, fp), lambda i: (i, 0)),
                  small((fp, ffn)), small((1, ffn)), small((1, ffn)),
                  small((ffn, ffn2)), small((1, ffn2)), small((1, ffn2)),
                  small((ffn2, oc)), small((1, oc)), small((1, oc))],
        out_specs=[pl.BlockSpec((tr, oc), lambda i: (i, 0)),
                   pl.BlockSpec((8, oc), lambda i: (0, 0))],
        out_shape=[jax.ShapeDtypeStruct((n, oc), BF),
                   jax.ShapeDtypeStruct((8, oc), F32)],
        compiler_params=_params("arbitrary"),
        name="hyena_filter_taps",
    )(feats, w1, b1, f1, w2, b2, f2, w3, decf, decb)
    return taps, ss


def _fnet_kernel(yr, yi, d2, csgw, b, o_ref, xcat, o_s, *, kb, n2, c):
    for k in range(kb):
        rows = pl.ds(k * n2, n2)
        cat = jnp.concatenate([yr[rows, :], yi[rows, :]], axis=0)
        x = jnp.dot(d2[...], cat, preferred_element_type=F32)
        xcat[rows, pl.ds(0, c)] = x[:n2].astype(BF)
        xcat[rows, pl.ds(c, c)] = x[n2:].astype(BF)
    out = jnp.dot(xcat[...], csgw[...], preferred_element_type=F32) + b[...]
    for k in range(kb):
        o_s[k] = out[k * n2:(k + 1) * n2].astype(o_s.dtype)
    o_ref[...] = pltpu.einshape("kmc->mkc", o_s[...])


def fnet_inner_stage(yr, yi, d2, csgw, b, n2, kb):
    bsz, l, c = yr.shape
    n1 = l // n2
    kb = min(kb, n1)
    spec = pl.BlockSpec((None, kb * n2, c), lambda bb, i: (bb, i, 0))
    const = lambda shape: pl.BlockSpec(shape, lambda bb, i: (0,) * len(shape))
    return pl.pallas_call(
        functools.partial(_fnet_kernel, kb=kb, n2=n2, c=c),
        grid=(bsz, n1 // kb),
        in_specs=[spec, spec, const((2 * n2, 2 * n2)), const((2 * c, c)), const((1, c))],
        out_specs=pl.BlockSpec((None, n2, kb, c), lambda bb, i: (bb, 0, i, 0)),
        out_shape=jax.ShapeDtypeStruct((bsz, n2, n1, c), BF),
        scratch_shapes=[pltpu.VMEM((kb * n2, 2 * c), BF), pltpu.VMEM((kb, n2, c), BF)],
        compiler_params=_params("parallel", "arbitrary"),
        name="fnet_inner",
    )(yr, yi, d2, csgw, b)


def _attn_kernel(q_ref, k_ref, v_ref, o_ref, *, heads):
    c = q_ref.shape[1]
    hd = c // heads
    scale = hd ** -0.5
    for h in range(heads):
        cols = pl.ds(h * hd, hd)
        s = lax.dot_general(q_ref[:, cols], k_ref[:, cols], (((1,), (1,)), ((), ())),
                            preferred_element_type=F32) * scale
        s = s - jnp.max(s, axis=-1, keepdims=True)
        e = jnp.exp(s)
        p = e / jnp.sum(e, axis=-1, keepdims=True)
        o = jnp.dot(p.astype(BF), v_ref[:, cols], preferred_element_type=F32)
        o_ref[:, cols] = o.astype(o_ref.dtype)


def memory_attention(q, kh, vh, seq, tr, col):
    m = q.shape[0]
    _, nm, c = kh.shape
    per = seq // tr
    kv = pl.BlockSpec((None, nm, c), lambda i: (i // per, 0, 0))
    return pl.pallas_call(
        functools.partial(_attn_kernel, heads=MEM_HEADS),
        grid=(m // tr,),
        in_specs=[pl.BlockSpec((tr, c), lambda i: (i, col)), kv, kv],
        out_specs=pl.BlockSpec((tr, c), lambda i: (i, 0)),
        out_shape=jax.ShapeDtypeStruct((m, c), BF),
        compiler_params=_params("parallel"),
        name="memory_attention",
    )(q, kh, vh)


def _merge_kernel(ya, yb, yc, ym, ga, gb, gc, gm, gg, w, x, pg, o_ref):
    g = ya.shape[1]
    acc = None
    for idx, (y_ref, gate) in enumerate(((ya, ga), (yb, gb), (yc, gc), (ym, gm))):
        cols = pl.ds(idx * g, g)
        y = y_ref[...].astype(F32)
        ms = jnp.mean(y * y, axis=-1, keepdims=True)
        yn = y * lax.rsqrt(ms + EPS) * gg[:, cols]
        gt = gate[...]
        yn = yn.astype(BF) * (gt * jax.nn.sigmoid(gt))
        part = jnp.dot(yn, w[cols, :], preferred_element_type=F32)
        acc = part if acc is None else acc + part
    ms = jnp.mean(acc * acc, axis=-1, keepdims=True)
    o_ref[...] = x[...] + acc * lax.rsqrt(ms + EPS) * pg[...]


def merge_project(ya, yb, yc, ym, gate, gate_col, gg, w_out, x, pg, tm):
    m, g = ya.shape
    d = x.shape[1]
    yspec = pl.BlockSpec((tm, g), lambda i: (i, 0))
    gspecs = [pl.BlockSpec((tm, g), lambda i, col=gate_col + k: (i, col)) for k in range(4)]
    const = lambda shape: pl.BlockSpec(shape, lambda i: (0, 0))
    return pl.pallas_call(
        _merge_kernel,
        grid=(m // tm,),
        in_specs=[yspec, yspec, yspec, yspec, *gspecs,
                  const((1, 4 * g)),
                  pl.BlockSpec((4 * g, d), lambda i: (0, 0), pipeline_mode=pl.Buffered(1)),
                  pl.BlockSpec((tm, d), lambda i: (i, 0)),
                  const((1, d))],
        out_specs=pl.BlockSpec((tm, d), lambda i: (i, 0)),
        out_shape=jax.ShapeDtypeStruct((m, d), F32),
        compiler_params=_params("parallel"),
        name="merge_project",
    )(ya, yb, yc, ym, gate, gate, gate, gate, gg, w_out, x, pg)


def _iota(n):
    return np.arange(n, dtype=np.int64)


@functools.lru_cache(maxsize=None)
def _outer_tables(n1, n2, rows_complex):
    n = n1 * n2
    r = _iota(n2)[:, None, None]
    k1 = _iota(n1)[None, :, None]
    nj = n1 // 2 if rows_complex else n1
    j = _iota(nj)[None, None, :]
    er, ei = _cis(k1 * (n2 * j + r), n, -1.0)
    if rows_complex:
        return _bf16_const(_stack_complex(er, ei))
    return _bf16_const(np.concatenate([er, ei], axis=-2))


@functools.lru_cache(maxsize=None)
def _inner_table(n2):
    k = _iota(n2)[:, None]
    j = _iota(n2)[None, :]
    cr, ci = _cis(k * j, n2, -1.0)
    return _bf16_const(_stack_complex(cr, ci))


@functools.lru_cache(maxsize=None)
def _inverse_inner_tables(n1, n2):
    n = n1 * n2
    k1 = _iota(n1)[:, None, None]
    na = _iota(n2)[None, :, None]
    k2 = _iota(n2)[None, None, :]
    ar, ai = _cis(na * (k1 + n1 * k2), n, 1.0)
    return _bf16_const(_stack_complex(ar, ai))


@functools.lru_cache(maxsize=None)
def _inverse_outer_table(n1, n2):
    nb = _iota(n1 // 2)[:, None]
    k1 = _iota(n1)[None, :]
    gr, gi = _cis(nb * k1, n1, 1.0)
    inv = 1.0 / (n1 * n2)
    return _bf16_const(_stack_complex(gr * inv, gi * inv))


@functools.lru_cache(maxsize=None)
def _channel_table(g, scale):
    a = _iota(g)[:, None]
    b = _iota(g)[None, :]
    cg, sg = _cis(a * b, g, 1.0)
    return _bf16_const(np.concatenate([cg, sg], axis=0) * scale)


def _filter_features(positions, seq, n1, n2):
    idx = (n2 * _iota(n1)[None, :] + _iota(n2)[:, None]).reshape(-1)
    pos = positions.astype(F32)
    pos = jnp.concatenate([pos, pos[:1], jnp.flip(pos[1:])])
    pos = pos.reshape(n1, n2).T.reshape(-1)
    t = pos / seq
    bands = jnp.linspace(1e-4, HYENA_BANDS - 1, HYENA_BANDS, dtype=F32)
    w = 2.0 * jnp.pi * pos / seq
    is_f = (idx < seq).astype(F32)
    is_b = (idx > seq).astype(F32)
    feats = jnp.concatenate([t[:, None], jnp.cos(w[:, None] * bands), jnp.sin(w[:, None] * bands),
                             is_f[:, None], is_b[:, None]], axis=-1)
    return jnp.pad(feats, ((0, 0), (0, FEAT_PAD - feats.shape[1])))


def _layer(x2, mem_n, feats, tabs, p, dims):
    bsz, seq, d, g = dims
    m = bsz * seq
    n2 = FFT_INNER
    n1 = 2 * seq // n2
    n1f, n2f = FFT_INNER, seq // FFT_INNER

    proj = input_projection(x2, p["pre_g"], p["w_in"].astype(BF), 1024, g)
    n_in = proj.shape[1]

    y_a = conv_module(proj, p["conv_dw_w"], p["conv_dw_b"], p["conv_ln_g"], p["conv_ln_b"],
                      p["conv_pw_w"], p["conv_pw_b"], seq, 512)

    yr, yi = dft_outer_stage(proj.reshape(bsz, n1f, n2f, n_in), tabs["fnet_outer"], DFT_RESIDUES, 512,
                             "fnet_outer", c=g, col0=2 * g)
    csgw = matmul(tabs["fnet_chan"], p["fnet_w"].astype(BF), 1024, 1024, "fnet_fold")
    y_b = fnet_inner_stage(yr.reshape(bsz, seq, g), yi.reshape(bsz, seq, g), tabs["fnet_inner"],
                           csgw, p["fnet_b"].reshape(1, g), n2f, 16)
    y_b = y_b.reshape(m, g)

    v, x1, x2g = hyena_short_conv(proj, p["hy_short_w"], p["hy_short_b"], seq, 512, 1)
    ffn = p["hy_fw1"].shape[1]
    og = HYENA_ORDER * g
    w1 = jnp.pad(p["hy_fw1"], ((0, FEAT_PAD - p["hy_fw1"].shape[0]), (0, 0)))
    dup = lambda a: jnp.concatenate([a, a], axis=-1)
    w3 = p["hy_fw3"].reshape(ffn, HYENA_ORDER, 2, g)
    w3 = jnp.concatenate([w3[:, :, 0, :].reshape(ffn, og), w3[:, :, 1, :].reshape(ffn, og)], axis=0)
    taps, ss = hyena_filter_taps(feats, w1, p["hy_fb1"].reshape(1, ffn), p["hy_freq1"].reshape(1, ffn),
                                 dup(p["hy_fw2"]), dup(p["hy_fb2"].reshape(1, ffn)), dup(p["hy_freq2"].reshape(1, ffn)),
                                 w3, p["hy_decay"][:, 0, :].reshape(1, og), p["hy_decay"][:, 1, :].reshape(1, og), 512)
    kur, kui = filter_outer_stage(taps, tabs["filt_outer"], n1, 4, 1024)
    kfr, kfi = filter_inner_stage(kur.reshape(n2, n1, og), kui.reshape(n2, n1, og), tabs["inner"], ss[0:1],
                                  DFT_RESIDUES, 1024)

    z = v
    for o, gate_o in enumerate((x1, x2g)):
        ur, ui = dft_outer_stage(z.reshape(1, n1, n2, g), tabs["hy_outer"], DFT_RESIDUES, 512, "hyena_outer_dft")
        qr, qi = hyena_mid_stage(ur.reshape(n1 * n2, g), ui.reshape(n1 * n2, g), kfr, kfi,
                                 tabs["inner"], tabs["inv_inner"], o, n2, 8)
        z = hyena_out_stage(qr.reshape(n1, n2, g), qi.reshape(n1, n2, g), tabs["inv_outer"],
                            gate_o.reshape(n1, n2, g), z.reshape(n1, n2, g), p["hy_skip"][o].reshape(1, g),
                            DFT_RESIDUES, 512)
    y_c = z.reshape(m, g)

    nm = mem_n.shape[0] // bsz
    kh = matmul(mem_n, p["mem_wk"].astype(BF), 512, 1024, "mem_k").reshape(bsz, nm, g)
    vh = matmul(mem_n, p["mem_wv"].astype(BF), 512, 1024, "mem_v").reshape(bsz, nm, g)
    y_m = memory_attention(proj, kh, vh, seq, 1024, 6)

    return merge_project(y_a, y_b, y_c, y_m, proj, 7, p["group_g"].reshape(1, 4 * g), p["w_out"].astype(BF),
                         x2, p["post_g"].reshape(1, d), 256)


def kernel(x, mem, positions, mem_norm_g, pre_norm_g, post_norm_g, w_in, conv_dw_w, conv_dw_b, conv_ln_g, conv_ln_b, conv_pw_w, conv_pw_b, fnet_w, fnet_b, hy_short_w, hy_short_b, hy_fw1, hy_fb1, hy_freq1, hy_fw2, hy_fb2, hy_freq2, hy_fw3, hy_decay, hy_skip, mem_wk, mem_wv, group_norm_g, w_out):
    bsz, seq, d = x.shape
    g = conv_pw_w.shape[-1]
    depth = w_in.shape[0]
    assert bsz == 2, "the long convolution packs exactly two batch elements as one complex sequence"
    assert seq % (FFT_INNER * DFT_RESIDUES) == 0
    n2 = FFT_INNER
    n1 = 2 * seq // n2
    n1f, n2f = FFT_INNER, seq // FFT_INNER

    tabs = {
        "fnet_outer": _outer_tables(n1f, n2f, False),
        "fnet_inner": _inner_table(n2f),
        "fnet_chan": _channel_table(g, 1.0 / math.sqrt(seq * g)),
        "hy_outer": _outer_tables(n1, n2, True),
        "filt_outer": _outer_tables(n1, n2, False),
        "inner": _inner_table(n2),
        "inv_inner": _inverse_inner_tables(n1, n2),
        "inv_outer": _inverse_outer_table(n1, n2),
    }
    tabs = {name: jnp.asarray(table) for name, table in tabs.items()}
    feats = _filter_features(positions, seq, n1, n2)
    mem_n = rmsnorm_bf16(mem.reshape(-1, d), mem_norm_g, min(512, mem.shape[0] * mem.shape[1]))

    x2 = x.reshape(bsz * seq, d)
    for l in range(depth):
        p = {
            "pre_g": pre_norm_g[l], "post_g": post_norm_g[l], "w_in": w_in[l],
            "conv_dw_w": conv_dw_w[l], "conv_dw_b": conv_dw_b[l], "conv_ln_g": conv_ln_g[l],
            "conv_ln_b": conv_ln_b[l], "conv_pw_w": conv_pw_w[l], "conv_pw_b": conv_pw_b[l],
            "fnet_w": fnet_w[l], "fnet_b": fnet_b[l], "hy_short_w": hy_short_w[l], "hy_short_b": hy_short_b[l],
            "hy_fw1": hy_fw1[l], "hy_fb1": hy_fb1[l], "hy_freq1": hy_freq1[l], "hy_fw2": hy_fw2[l],
            "hy_fb2": hy_fb2[l], "hy_freq2": hy_freq2[l], "hy_fw3": hy_fw3[l], "hy_decay": hy_decay[l],
            "hy_skip": hy_skip[l], "mem_wk": mem_wk[l], "mem_wv": mem_wv[l], "group_g": group_norm_g[l],
            "w_out": w_out[l],
        }
        x2 = _layer(x2, mem_n, feats, tabs, p, (bsz, seq, d, g))
    return x2.reshape(bsz, seq, d)
```

```python
import functools
import math

import jax
import jax.numpy as jnp
import numpy as np
from jax import lax
from jax.experimental import pallas as pl
from jax.experimental.pallas import tpu as pltpu

BF = jnp.bfloat16
F32 = jnp.float32
EPS = 1e-6

CONV_WIDTH = 31
CONV_HALF = CONV_WIDTH // 2
HALO = 16
SUBLANES, LANES = 8, 128
HYENA_BANDS = 16
HYENA_ORDER = 2
MEM_HEADS = 4
FFT_INNER = 128
FEAT_PAD = 128
FWD_COL = 2 * HYENA_BANDS + 1
BWD_COL = 2 * HYENA_BANDS + 2
DFT_RESIDUES = 16
VMEM_LIMIT = 56 * 1024 * 1024


def _params(*sem):
    return pltpu.CompilerParams(dimension_semantics=sem, vmem_limit_bytes=VMEM_LIMIT)


def _rmsnorm_kernel(x_ref, g_ref, o_ref):
    x = x_ref[...]
    ms = jnp.mean(x * x, axis=-1, keepdims=True)
    o_ref[...] = (x * lax.rsqrt(ms + EPS) * g_ref[...]).astype(o_ref.dtype)


def rmsnorm_bf16(x, g, tm):
    m, d = x.shape
    return pl.pallas_call(
        _rmsnorm_kernel,
        grid=(m // tm,),
        in_specs=[pl.BlockSpec((tm, d), lambda i: (i, 0)),
                  pl.BlockSpec((1, d), lambda i: (0, 0))],
        out_specs=pl.BlockSpec((tm, d), lambda i: (i, 0)),
        out_shape=jax.ShapeDtypeStruct((m, d), BF),
        compiler_params=_params("parallel"),
        name="rmsnorm",
    )(x, g.reshape(1, d))


def _mm_kernel(a_ref, w_ref, o_ref):
    o_ref[...] = jnp.dot(a_ref[...], w_ref[...], preferred_element_type=F32).astype(o_ref.dtype)


def matmul(a, w, tm, tn, name, col0=0, n=None):
    m, k = a.shape
    n = w.shape[1] if n is None else n
    tm, tn = min(tm, m), min(tn, n)
    joff = col0 // tn
    return pl.pallas_call(
        _mm_kernel,
        grid=(m // tm, n // tn),
        in_specs=[pl.BlockSpec((tm, k), lambda i, j: (i, 0)),
                  pl.BlockSpec((k, tn), lambda i, j: (0, j + joff))],
        out_specs=pl.BlockSpec((tm, tn), lambda i, j: (i, j)),
        out_shape=jax.ShapeDtypeStruct((m, n), BF),
        compiler_params=_params("parallel", "arbitrary"),
        name=name,
    )(a, w)


def _proj_kernel(x_ref, g_ref, w_ref, o_ref, h_s):
    @pl.when(pl.program_id(1) == 0)
    def _():
        x = x_ref[...]
        ms = jnp.mean(x * x, axis=-1, keepdims=True)
        h_s[...] = (x * lax.rsqrt(ms + EPS) * g_ref[...]).astype(h_s.dtype)

    o_ref[...] = jnp.dot(h_s[...], w_ref[...], preferred_element_type=F32).astype(o_ref.dtype)


def input_projection(x, g, w, tm, tn):
    m, d = x.shape
    n = w.shape[1]
    return pl.pallas_call(
        _proj_kernel,
        grid=(m // tm, n // tn),
        in_specs=[pl.BlockSpec((tm, d), lambda i, j: (i, 0)),
                  pl.BlockSpec((1, d), lambda i, j: (0, 0)),
                  pl.BlockSpec((d, tn), lambda i, j: (0, j))],
        out_specs=pl.BlockSpec((tm, tn), lambda i, j: (i, j)),
        out_shape=jax.ShapeDtypeStruct((m, n), BF),
        scratch_shapes=[pltpu.VMEM((tm, d), BF)],
        compiler_params=_params("parallel", "arbitrary"),
        name="input_projection",
    )(x, g.reshape(1, d), w)


def _halo_specs(tr, width, col):
    per = tr // HALO

    def cur(i):
        return (i, col)

    def prev(i):
        return (jnp.maximum(i * per - 1, 0), col)

    def make_next(nblocks):
        def nxt(i):
            return (jnp.minimum((i + 1) * per, nblocks - 1), col)
        return nxt

    return cur, prev, make_next


def _seq_edges(i, tr, seq):
    tiles = seq // tr
    pos = i % tiles
    return pos == 0, pos == tiles - 1


def _conva_kernel(vc, gc, vp, gp, vn, gn, dww, dwb, lng, lnb, pww, pwb, o_ref, upad, cbuf, wbc, *, seq):
    tr, c = vc.shape
    first, last = _seq_edges(pl.program_id(0), tr, seq)

    def glu(v, g):
        return v[...].astype(F32) * jax.nn.sigmoid(g[...].astype(F32))

    upad[pl.ds(HALO, tr), :] = glu(vc, gc)
    upad[pl.ds(0, HALO), :] = jnp.where(first, 0.0, glu(vp, gp))
    upad[pl.ds(HALO + tr, HALO), :] = jnp.where(last, 0.0, glu(vn, gn))

    for j in range(CONV_WIDTH):
        wbc[pl.ds(SUBLANES * j, SUBLANES), :] = jnp.broadcast_to(dww[pl.ds(j, 1), :], (SUBLANES, c))
    row = lax.broadcasted_iota(jnp.int32, (SUBLANES, LANES), 0)
    for ct in range(c // LANES):
        lanes = pl.ds(ct * LANES, LANES)

        def residue_sums(b, lanes=lanes):
            u = [upad[pl.ds(pl.multiple_of((b + a) * SUBLANES, SUBLANES), SUBLANES), lanes] for a in range(4)]
            sums = []
            for s in range(SUBLANES):
                acc = None
                for a in range(4):
                    j = SUBLANES * a + s - 1
                    if 0 <= j < CONV_WIDTH:
                        term = u[a] * wbc[pl.ds(SUBLANES * j, SUBLANES), lanes]
                        acc = term if acc is None else acc + term
                sums.append(acc)
            return tuple(sums)

        bias = jnp.broadcast_to(dwb[:, lanes], (SUBLANES, LANES))

        def body(b, prev, lanes=lanes, bias=bias, residue_sums=residue_sums):
            cur = residue_sums(b)
            terms = [bias + prev[0]]
            for s in range(1, SUBLANES):
                terms.append(pltpu.roll(jnp.where(row >= s, prev[s], cur[s]), SUBLANES - s, axis=0))
            while len(terms) > 1:
                terms = [terms[i] + terms[i + 1] for i in range(0, len(terms), 2)]
            out = terms[0]
            cbuf[pl.ds(pl.multiple_of((b - 1) * SUBLANES, SUBLANES), SUBLANES), lanes] = out
            return cur

        lax.fori_loop(1, tr // SUBLANES + 1, body, residue_sums(0), unroll=4)

    u = cbuf[...]
    mu = jnp.mean(u, axis=-1, keepdims=True)
    d = u - mu
    var = jnp.mean(d * d, axis=-1, keepdims=True)
    y = d * lax.rsqrt(var + EPS) * lng[...] + lnb[...]
    y = y * jax.nn.sigmoid(y)
    out = jnp.dot(y.astype(BF), pww[...], preferred_element_type=F32) + pwb[...]
    o_ref[...] = out.astype(o_ref.dtype)


def conv_module(a, dww, dwb, lng, lnb, pww, pwb, seq, tr):
    m = a.shape[0]
    c = pww.shape[0]
    cur, prev, make_next = _halo_specs(tr, c, 0)
    cur_g, prev_g, make_next_g = _halo_specs(tr, c, 1)
    nh = m // HALO
    row = lambda i: (0, 0)
    return pl.pallas_call(
        functools.partial(_conva_kernel, seq=seq),
        grid=(m // tr,),
        in_specs=[pl.BlockSpec((tr, c), cur), pl.BlockSpec((tr, c), cur_g),
                  pl.BlockSpec((HALO, c), prev), pl.BlockSpec((HALO, c), prev_g),
                  pl.BlockSpec((HALO, c), make_next(nh)), pl.BlockSpec((HALO, c), make_next_g(nh)),
                  pl.BlockSpec((CONV_WIDTH, c), row), pl.BlockSpec((1, c), row),
                  pl.BlockSpec((1, c), row), pl.BlockSpec((1, c), row),
                  pl.BlockSpec((c, c), row), pl.BlockSpec((1, c), row)],
        out_specs=pl.BlockSpec((tr, c), lambda i: (i, 0)),
        out_shape=jax.ShapeDtypeStruct((m, c), BF),
        scratch_shapes=[pltpu.VMEM((tr + 2 * HALO, c), F32), pltpu.VMEM((tr, c), F32),
                        pltpu.VMEM((CONV_WIDTH * SUBLANES, c), F32)],
        compiler_params=_params("parallel"),
        name="conv_module",
    )(a, a, a, a, a, a, dww, dwb.reshape(1, c), lng.reshape(1, c), lnb.reshape(1, c),
      pww.astype(BF), pwb.reshape(1, c))


def _short_kernel(hc, hp, hn, w, b, v_ref, x1_ref, x2_ref, *, seq):
    tr, c3 = hc.shape
    c = c3 // 3
    first, last = _seq_edges(pl.program_id(0), tr, seq)
    edge = lax.broadcasted_iota(jnp.int32, (HALO, c), 0)
    for g, o_ref in enumerate((v_ref, x1_ref, x2_ref)):
        cols = pl.ds(g * c, c)
        w0, w1, w2, bias = w[pl.ds(0, 1), cols], w[pl.ds(1, 1), cols], w[pl.ds(2, 1), cols], b[:, cols]
        cur = hc[:, cols].astype(F32)
        before = pltpu.roll(cur, 1, axis=0)
        after = pltpu.roll(cur, tr - 1, axis=0)
        o_ref[...] = (before * w0 + cur * w1 + after * w2 + bias).astype(o_ref.dtype)

        prev_row = jnp.where(first, 0.0, hp[pl.ds(HALO - 1, 1), cols].astype(F32))
        next_row = jnp.where(last, 0.0, hn[pl.ds(0, 1), cols].astype(F32))
        head = jnp.where(edge == 0, prev_row, before[:HALO])
        tail = jnp.where(edge == HALO - 1, next_row, after[tr - HALO:])
        o_ref[pl.ds(0, HALO), :] = (head * w0 + cur[:HALO] * w1 + after[:HALO] * w2 + bias).astype(o_ref.dtype)
        o_ref[pl.ds(tr - HALO, HALO), :] = (before[tr - HALO:] * w0 + cur[tr - HALO:] * w1 + tail * w2
                                            + bias).astype(o_ref.dtype)


def hyena_short_conv(hy, w, b, seq, tr, col):
    m = hy.shape[0]
    c3 = w.shape[1]
    c = c3 // 3
    cur, prev, make_next = _halo_specs(tr, c3, col)
    out = jax.ShapeDtypeStruct((m, c), BF)
    ospec = pl.BlockSpec((tr, c), lambda i: (i, 0))
    return pl.pallas_call(
        functools.partial(_short_kernel, seq=seq),
        grid=(m // tr,),
        in_specs=[pl.BlockSpec((tr, c3), cur), pl.BlockSpec((HALO, c3), prev),
                  pl.BlockSpec((HALO, c3), make_next(m // HALO)),
                  pl.BlockSpec((3, c3), lambda i: (0, 0)), pl.BlockSpec((1, c3), lambda i: (0, 0))],
        out_specs=[ospec, ospec, ospec],
        out_shape=[out, out, out],
        compiler_params=_params("parallel"),
        name="hyena_short_conv",
    )(hy, hy, hy, w, b.reshape(1, c3))


def _cis(p, n, sign):
    ang = (2.0 * np.pi / n) * (p % n)
    return np.cos(ang), sign * np.sin(ang)


def _stack_complex(mr, mi):
    return np.concatenate([np.concatenate([mr, -mi], axis=-1),
                           np.concatenate([mi, mr], axis=-1)], axis=-2)


def _bf16_const(a):
    return a.astype(jnp.bfloat16)


def _outer_stage_kernel(x_ref, t_ref, or_ref, oi_ref, yr_s, yi_s, *, nb):
    xt = pltpu.einshape("nrc->rnc", x_ref[...])
    for r in range(nb):
        y = jnp.dot(t_ref[r], xt[r], preferred_element_type=F32)
        h = y.shape[0] // 2
        yr_s[r] = y[:h].astype(BF)
        yi_s[r] = y[h:].astype(BF)
    or_ref[...] = pltpu.einshape("rnc->nrc", yr_s[...])
    oi_ref[...] = pltpu.einshape("rnc->nrc", yi_s[...])


def dft_outer_stage(x, table, nb, tc, name, c=None, col0=0):
    bsz, rows, n_inner, _ = x.shape
    c = x.shape[3] if c is None else c
    _, two_k, _ = table.shape
    kout = two_k // 2
    tc = min(tc, c)
    coff = col0 // tc
    out = jax.ShapeDtypeStruct((bsz, kout, n_inner, c), BF)
    ospec = pl.BlockSpec((None, kout, nb, tc), lambda b, j, cc: (b, 0, j, cc))
    return pl.pallas_call(
        functools.partial(_outer_stage_kernel, nb=nb),
        grid=(bsz, n_inner // nb, c // tc),
        in_specs=[pl.BlockSpec((None, rows, nb, tc), lambda b, j, cc: (b, 0, j, cc + coff)),
                  pl.BlockSpec((nb, two_k, rows), lambda b, j, cc: (j, 0, 0))],
        out_specs=[ospec, ospec],
        out_shape=[out, out],
        scratch_shapes=[pltpu.VMEM((nb, kout, tc), BF), pltpu.VMEM((nb, kout, tc), BF)],
        compiler_params=_params("parallel", "arbitrary", "arbitrary"),
        name=name,
    )(x, table)


def _filter_inner_kernel(ur, ui, d, ss, kr_ref, ki_ref, *, kb, n2):
    scale = lax.rsqrt(ss[...] + EPS)
    urt = pltpu.einshape("nkc->knc", ur[...])
    uit = pltpu.einshape("nkc->knc", ui[...])
    for k in range(kb):
        rows = pl.ds(k * n2, n2)
        cat = jnp.concatenate([urt[k], uit[k]], axis=0)
        x = jnp.dot(d[...], cat, preferred_element_type=F32) * scale
        kr_ref[rows, :] = x[:n2].astype(kr_ref.dtype)
        ki_ref[rows, :] = x[n2:].astype(ki_ref.dtype)


def filter_inner_stage(ur, ui, d, ss, kb, tc):
    n2, n1, c = ur.shape
    tc = min(tc, c)
    ispec = pl.BlockSpec((n2, kb, tc), lambda i, cc: (0, i, cc))
    ospec = pl.BlockSpec((kb * n2, tc), lambda i, cc: (i, cc))
    out = jax.ShapeDtypeStruct((n1 * n2, c), BF)
    return pl.pallas_call(
        functools.partial(_filter_inner_kernel, kb=kb, n2=n2),
        grid=(n1 // kb, c // tc),
        in_specs=[ispec, ispec, pl.BlockSpec((2 * n2, 2 * n2), lambda i, cc: (0, 0)),
                  pl.BlockSpec((1, tc), lambda i, cc: (0, cc))],
        out_specs=[ospec, ospec],
        out_shape=[out, out],
        compiler_params=_params("parallel", "arbitrary"),
        name="filter_inner_dft",
    )(ur, ui, d, ss)


def _hyena_mid_kernel(ur, ui, kr, ki, d, minv, qr_ref, qi_ref, *, kb, n2):
    for k in range(kb):
        rows = pl.ds(k * n2, n2)
        cat = jnp.concatenate([ur[rows, :], ui[rows, :]], axis=0)
        x = jnp.dot(d[...], cat, preferred_element_type=F32)
        xr, xi = x[:n2], x[n2:]
        fr, fi = kr[rows, :].astype(F32), ki[rows, :].astype(F32)
        pr = xr * fr - xi * fi
        pi = xr * fi + xi * fr
        catp = jnp.concatenate([pr, pi], axis=0).astype(BF)
        q = jnp.dot(minv[k], catp, preferred_element_type=F32)
        qr_ref[rows, :] = q[:n2].astype(qr_ref.dtype)
        qi_ref[rows, :] = q[n2:].astype(qi_ref.dtype)


def hyena_mid_stage(ur, ui, kfr, kfi, d, minv, order, n2, kb):
    n, c = ur.shape
    n1 = n // n2
    kb = min(kb, n1)
    spec = pl.BlockSpec((kb * n2, c), lambda i: (i, 0))
    kspec = pl.BlockSpec((kb * n2, c), lambda i: (i, order))
    out = jax.ShapeDtypeStruct((n, c), BF)
    return pl.pallas_call(
        functools.partial(_hyena_mid_kernel, kb=kb, n2=n2),
        grid=(n1 // kb,),
        in_specs=[spec, spec, kspec, kspec,
                  pl.BlockSpec((2 * n2, 2 * n2), lambda i: (0, 0)),
                  pl.BlockSpec((kb, 2 * n2, 2 * n2), lambda i: (i, 0, 0))],
        out_specs=[spec, spec],
        out_shape=[out, out],
        compiler_params=_params("parallel"),
        name="hyena_mid",
    )(ur, ui, kfr, kfi, d, minv)


def _hyena_out_kernel(qr, qi, g, xo, z, skip, o_ref, y_s, *, nb):
    qrt = pltpu.einshape("nrc->rnc", qr[...])
    qit = pltpu.einshape("nrc->rnc", qi[...])
    for r in range(nb):
        cat = jnp.concatenate([qrt[r], qit[r]], axis=0)
        y_s[r] = jnp.dot(g[...], cat, preferred_element_type=F32)
    y = pltpu.einshape("rnc->nrc", y_s[...])
    zf = z[...].astype(F32)
    o_ref[...] = (xo[...].astype(F32) * (y + skip[...] * zf)).astype(o_ref.dtype)


def hyena_out_stage(qr, qi, g, xo, z, skip, nb, tc):
    rows, n2, c = qr.shape
    tc = min(tc, c)
    spec = pl.BlockSpec((rows, nb, tc), lambda j, cc: (0, j, cc))
    return pl.pallas_call(
        functools.partial(_hyena_out_kernel, nb=nb),
        grid=(n2 // nb, c // tc),
        in_specs=[spec, spec, pl.BlockSpec((rows, 2 * rows), lambda j, cc: (0, 0)), spec, spec,
                  pl.BlockSpec((1, tc), lambda j, cc: (0, cc))],
        out_specs=spec,
        out_shape=jax.ShapeDtypeStruct((rows, n2, c), BF),
        scratch_shapes=[pltpu.VMEM((nb, rows, tc), F32)],
        compiler_params=_params("parallel", "arbitrary"),
        name="hyena_out",
    )(qr, qi, g, xo, z, skip)


def _dot_bf16x3(a, b):
    ah = a.astype(BF)
    bh = b.astype(BF)
    al = (a - ah.astype(F32)).astype(BF)
    bl = (b - bh.astype(F32)).astype(BF)
    dot = functools.partial(jnp.dot, preferred_element_type=F32)
    return dot(ah, bh) + dot(al, bh) + dot(ah, bl)


def _filter_kernel(ft, w1, b1, f1, w2, b2, f2, w3, decf, decb, tab, ur_ref, ui_ref, ss_ref, *, n1):
    hi = lax.Precision.HIGHEST
    x = ft[...]
    ffn2 = w2.shape[1]
    h = jnp.sin(f1[...] * (jnp.dot(x, w1[...], precision=hi, preferred_element_type=F32) + b1[...]))
    h = jnp.sin(f2[...] * (jnp.dot(h, w2[...], precision=hi, preferred_element_type=F32) + b2[...]))
    t = x[:, 0:1]
    is_f = x[:, FWD_COL:FWD_COL + 1]
    is_b = x[:, BWD_COL:BWD_COL + 1]
    lane = lax.broadcasted_iota(jnp.int32, h.shape, 1)
    h = h * jnp.where(lane < ffn2 // 2, is_f, is_b)
    h = _dot_bf16x3(h, w3[...])
    dec = is_f * jnp.abs(decf[...]) + is_b * jnp.abs(decb[...])
    k = h * jnp.exp(-t * dec)
    kb = k.astype(BF)
    for r in range(tab.shape[0]):
        rows = pl.ds(r * n1, n1)
        u = jnp.dot(tab[r], kb[r * n1:(r + 1) * n1], preferred_element_type=F32)
        ur_ref[rows, :] = u[:n1].astype(ur_ref.dtype)
        ui_ref[rows, :] = u[n1:].astype(ui_ref.dtype)
    part = jnp.sum(k * k, axis=0, keepdims=True)

    @pl.when(pl.program_id(0) == 0)
    def _():
        ss_ref[...] = jnp.zeros_like(ss_ref)

    ss_ref[...] += jnp.broadcast_to(part, ss_ref.shape)


def hyena_filter_taps(feats, w1, b1, f1, w2, b2, f2, w3, decf, decb, table, nb):
    n, fp = feats.shape
    ffn = w1.shape[1]
    ffn2 = w2.shape[1]
    oc = w3.shape[-1]
    _, two_n1, n1 = table.shape
    tr = nb * n1
    small = lambda shape: pl.BlockSpec(shape, lambda i: (0,) * len(shape))
    ospec = pl.BlockSpec((tr, oc), lambda i: (i, 0))
    out = jax.ShapeDtypeStruct((n, oc), BF)
    return pl.pallas_call(
        functools.partial(_filter_kernel, n1=n1),
        grid=(n // tr,),
        in_specs=[pl.BlockSpec((tr, fp), lambda i: (i, 0)),
                  small((fp, ffn)), small((1, ffn)), small((1, ffn)),
                  small((ffn, ffn2)), small((1, ffn2)), small((1, ffn2)),
                  small((ffn2, oc)), small((1, oc)), small((1, oc)),
                  pl.BlockSpec((nb, two_n1, n1), lambda i: (i, 0, 0))],
        out_specs=[ospec, ospec, pl.BlockSpec((8, oc), lambda i: (0, 0))],
        out_shape=[out, out, jax.ShapeDtypeStruct((8, oc), F32)],
        compiler_params=_params("arbitrary"),
        name="hyena_filter_taps",
    )(feats, w1, b1, f1, w2, b2, f2, w3, decf, decb, table)


def _fnet_kernel(yr, yi, d2, csgw, b, o_ref, xcat, o_s, *, kb, n2, c):
    for k in range(kb):
        rows = pl.ds(k * n2, n2)
        cat = jnp.concatenate([yr[rows, :], yi[rows, :]], axis=0)
        x = jnp.dot(d2[...], cat, preferred_element_type=F32)
        xcat[rows, pl.ds(0, c)] = x[:n2].astype(BF)
        xcat[rows, pl.ds(c, c)] = x[n2:].astype(BF)
    out = jnp.dot(xcat[...], csgw[...], preferred_element_type=F32) + b[...]
    for k in range(kb):
        o_s[k] = out[k * n2:(k + 1) * n2].astype(o_s.dtype)
    o_ref[...] = pltpu.einshape("kmc->mkc", o_s[...])


def fnet_inner_stage(yr, yi, d2, csgw, b, n2, kb):
    bsz, l, c = yr.shape
    n1 = l // n2
    kb = min(kb, n1)
    spec = pl.BlockSpec((None, kb * n2, c), lambda bb, i: (bb, i, 0))
    const = lambda shape: pl.BlockSpec(shape, lambda bb, i: (0,) * len(shape))
    return pl.pallas_call(
        functools.partial(_fnet_kernel, kb=kb, n2=n2, c=c),
        grid=(bsz, n1 // kb),
        in_specs=[spec, spec, const((2 * n2, 2 * n2)), const((2 * c, c)), const((1, c))],
        out_specs=pl.BlockSpec((None, n2, kb, c), lambda bb, i: (bb, 0, i, 0)),
        out_shape=jax.ShapeDtypeStruct((bsz, n2, n1, c), BF),
        scratch_shapes=[pltpu.VMEM((kb * n2, 2 * c), BF), pltpu.VMEM((kb, n2, c), BF)],
        compiler_params=_params("parallel", "arbitrary"),
        name="fnet_inner",
    )(yr, yi, d2, csgw, b)


def _attn_kernel(q_ref, k_ref, v_ref, o_ref, *, heads):
    c = q_ref.shape[1]
    hd = c // heads
    scale = hd ** -0.5
    for h in range(heads):
        cols = pl.ds(h * hd, hd)
        s = lax.dot_general(q_ref[:, cols], k_ref[:, cols], (((1,), (1,)), ((), ())),
                            preferred_element_type=F32) * scale
        s = s - jnp.max(s, axis=-1, keepdims=True)
        e = jnp.exp(s)
        p = e / jnp.sum(e, axis=-1, keepdims=True)
        o = jnp.dot(p.astype(BF), v_ref[:, cols], preferred_element_type=F32)
        o_ref[:, cols] = o.astype(o_ref.dtype)


def memory_attention(q, kh, vh, seq, tr, col):
    m = q.shape[0]
    _, nm, c = kh.shape
    per = seq // tr
    kv = pl.BlockSpec((None, nm, c), lambda i: (i // per, 0, 0))
    return pl.pallas_call(
        functools.partial(_attn_kernel, heads=MEM_HEADS),
        grid=(m // tr,),
        in_specs=[pl.BlockSpec((tr, c), lambda i: (i, col)), kv, kv],
        out_specs=pl.BlockSpec((tr, c), lambda i: (i, 0)),
        out_shape=jax.ShapeDtypeStruct((m, c), BF),
        compiler_params=_params("parallel"),
        name="memory_attention",
    )(q, kh, vh)


def _merge_kernel(ya, yb, yc, ym, ga, gb, gc, gm, gg, w, x, pg, o_ref):
    g = ya.shape[1]
    acc = None
    for idx, (y_ref, gate) in enumerate(((ya, ga), (yb, gb), (yc, gc), (ym, gm))):
        cols = pl.ds(idx * g, g)
        y = y_ref[...].astype(F32)
        ms = jnp.mean(y * y, axis=-1, keepdims=True)
        yn = y * lax.rsqrt(ms + EPS) * gg[:, cols]
        gt = gate[...]
        yn = yn.astype(BF) * (gt * jax.nn.sigmoid(gt))
        part = jnp.dot(yn, w[cols, :], preferred_element_type=F32)
        acc = part if acc is None else acc + part
    ms = jnp.mean(acc * acc, axis=-1, keepdims=True)
    o_ref[...] = x[...] + acc * lax.rsqrt(ms + EPS) * pg[...]


def merge_project(ya, yb, yc, ym, gate, gate_col, gg, w_out, x, pg, tm):
    m, g = ya.shape
    d = x.shape[1]
    yspec = pl.BlockSpec((tm, g), lambda i: (i, 0))
    gspecs = [pl.BlockSpec((tm, g), lambda i, col=gate_col + k: (i, col)) for k in range(4)]
    const = lambda shape: pl.BlockSpec(shape, lambda i: (0, 0))
    return pl.pallas_call(
        _merge_kernel,
        grid=(m // tm,),
        in_specs=[yspec, yspec, yspec, yspec, *gspecs,
                  const((1, 4 * g)),
                  pl.BlockSpec((4 * g, d), lambda i: (0, 0), pipeline_mode=pl.Buffered(1)),
                  pl.BlockSpec((tm, d), lambda i: (i, 0)),
                  const((1, d))],
        out_specs=pl.BlockSpec((tm, d), lambda i: (i, 0)),
        out_shape=jax.ShapeDtypeStruct((m, d), F32),
        compiler_params=_params("parallel"),
        name="merge_project",
    )(ya, yb, yc, ym, gate, gate, gate, gate, gg, w_out, x, pg)


def _iota(n):
    return np.arange(n, dtype=np.int64)


@functools.lru_cache(maxsize=None)
def _outer_tables(n1, n2, rows_complex):
    n = n1 * n2
    r = _iota(n2)[:, None, None]
    k1 = _iota(n1)[None, :, None]
    nj = n1 // 2 if rows_complex else n1
    j = _iota(nj)[None, None, :]
    er, ei = _cis(k1 * (n2 * j + r), n, -1.0)
    if rows_complex:
        return _bf16_const(_stack_complex(er, ei))
    return _bf16_const(np.concatenate([er, ei], axis=-2))


@functools.lru_cache(maxsize=None)
def _inner_table(n2):
    k = _iota(n2)[:, None]
    j = _iota(n2)[None, :]
    cr, ci = _cis(k * j, n2, -1.0)
    return _bf16_const(_stack_complex(cr, ci))


@functools.lru_cache(maxsize=None)
def _inverse_inner_tables(n1, n2):
    n = n1 * n2
    k1 = _iota(n1)[:, None, None]
    na = _iota(n2)[None, :, None]
    k2 = _iota(n2)[None, None, :]
    ar, ai = _cis(na * (k1 + n1 * k2), n, 1.0)
    return _bf16_const(_stack_complex(ar, ai))


@functools.lru_cache(maxsize=None)
def _inverse_outer_table(n1, n2):
    nb = _iota(n1 // 2)[:, None]
    k1 = _iota(n1)[None, :]
    gr, gi = _cis(nb * k1, n1, 1.0)
    inv = 1.0 / (n1 * n2)
    return _bf16_const(_stack_complex(gr * inv, gi * inv))


@functools.lru_cache(maxsize=None)
def _channel_table(g, scale):
    a = _iota(g)[:, None]
    b = _iota(g)[None, :]
    cg, sg = _cis(a * b, g, 1.0)
    return _bf16_const(np.concatenate([cg, sg], axis=0) * scale)


def _filter_features(positions, seq, n1, n2):
    idx = (n2 * _iota(n1)[None, :] + _iota(n2)[:, None]).reshape(-1)
    pos = positions.astype(F32)
    pos = jnp.concatenate([pos, pos[:1], jnp.flip(pos[1:])])
    pos = pos.reshape(n1, n2).T.reshape(-1)
    t = pos / seq
    bands = jnp.linspace(1e-4, HYENA_BANDS - 1, HYENA_BANDS, dtype=F32)
    w = 2.0 * jnp.pi * pos / seq
    is_f = (idx < seq).astype(F32)
    is_b = (idx > seq).astype(F32)
    feats = jnp.concatenate([t[:, None], jnp.cos(w[:, None] * bands), jnp.sin(w[:, None] * bands),
                             is_f[:, None], is_b[:, None]], axis=-1)
    return jnp.pad(feats, ((0, 0), (0, FEAT_PAD - feats.shape[1])))


def _layer(x2, mem_n, feats, tabs, p, dims):
    bsz, seq, d, g = dims
    m = bsz * seq
    n2 = FFT_INNER
    n1 = 2 * seq // n2
    n1f, n2f = FFT_INNER, seq // FFT_INNER

    proj = input_projection(x2, p["pre_g"], p["w_in"].astype(BF), 1024, g)
    n_in = proj.shape[1]

    y_a = conv_module(proj, p["conv_dw_w"], p["conv_dw_b"], p["conv_ln_g"], p["conv_ln_b"],
                      p["conv_pw_w"], p["conv_pw_b"], seq, 512)

    yr, yi = dft_outer_stage(proj.reshape(bsz, n1f, n2f, n_in), tabs["fnet_outer"], DFT_RESIDUES, 512,
                             "fnet_outer", c=g, col0=2 * g)
    csgw = matmul(tabs["fnet_chan"], p["fnet_w"].astype(BF), 1024, 1024, "fnet_fold")
    y_b = fnet_inner_stage(yr.reshape(bsz, seq, g), yi.reshape(bsz, seq, g), tabs["fnet_inner"],
                           csgw, p["fnet_b"].reshape(1, g), n2f, 16)
    y_b = y_b.reshape(m, g)

    v, x1, x2g = hyena_short_conv(proj, p["hy_short_w"], p["hy_short_b"], seq, 512, 1)
    ffn = p["hy_fw1"].shape[1]
    og = HYENA_ORDER * g
    w1 = jnp.pad(p["hy_fw1"], ((0, FEAT_PAD - p["hy_fw1"].shape[0]), (0, 0)))
    dup = lambda a: jnp.concatenate([a, a], axis=-1)
    w3 = p["hy_fw3"].reshape(ffn, HYENA_ORDER, 2, g)
    w3 = jnp.concatenate([w3[:, :, 0, :].reshape(ffn, og), w3[:, :, 1, :].reshape(ffn, og)], axis=0)
    kur, kui, ss = hyena_filter_taps(
        feats, w1, p["hy_fb1"].reshape(1, ffn), p["hy_freq1"].reshape(1, ffn),
        dup(p["hy_fw2"]), dup(p["hy_fb2"].reshape(1, ffn)), dup(p["hy_freq2"].reshape(1, ffn)),
        w3, p["hy_decay"][:, 0, :].reshape(1, og), p["hy_decay"][:, 1, :].reshape(1, og),
        tabs["filt_outer"], max(1, 512 // n1))
    kfr, kfi = filter_inner_stage(kur.reshape(n2, n1, og), kui.reshape(n2, n1, og), tabs["inner"], ss[0:1],
                                  DFT_RESIDUES, 1024)

    z = v
    for o, gate_o in enumerate((x1, x2g)):
        ur, ui = dft_outer_stage(z.reshape(1, n1, n2, g), tabs["hy_outer"], DFT_RESIDUES, 512, "hyena_outer_dft")
        qr, qi = hyena_mid_stage(ur.reshape(n1 * n2, g), ui.reshape(n1 * n2, g), kfr, kfi,
                                 tabs["inner"], tabs["inv_inner"], o, n2, 8)
        z = hyena_out_stage(qr.reshape(n1, n2, g), qi.reshape(n1, n2, g), tabs["inv_outer"],
                            gate_o.reshape(n1, n2, g), z.reshape(n1, n2, g), p["hy_skip"][o].reshape(1, g),
                            DFT_RESIDUES, 512)
    y_c = z.reshape(m, g)

    nm = mem_n.shape[0] // bsz
    kh = matmul(mem_n, p["mem_wk"].astype(BF), 512, 1024, "mem_k").reshape(bsz, nm, g)
    vh = matmul(mem_n, p["mem_wv"].astype(BF), 512, 1024, "mem_v").reshape(bsz, nm, g)
    y_m = memory_attention(proj, kh, vh, seq, 1024, 6)

    return merge_project(y_a, y_b, y_c, y_m, proj, 7, p["group_g"].reshape(1, 4 * g), p["w_out"].astype(BF),
                         x2, p["post_g"].reshape(1, d), 256)


def kernel(x, mem, positions, mem_norm_g, pre_norm_g, post_norm_g, w_in, conv_dw_w, conv_dw_b, conv_ln_g, conv_ln_b, conv_pw_w, conv_pw_b, fnet_w, fnet_b, hy_short_w, hy_short_b, hy_fw1, hy_fb1, hy_freq1, hy_fw2, hy_fb2, hy_freq2, hy_fw3, hy_decay, hy_skip, mem_wk, mem_wv, group_norm_g, w_out):
    bsz, seq, d = x.shape
    g = conv_pw_w.shape[-1]
    depth = w_in.shape[0]
    assert bsz == 2, "the long convolution packs exactly two batch elements as one complex sequence"
    assert seq % (FFT_INNER * DFT_RESIDUES) == 0
    n2 = FFT_INNER
    n1 = 2 * seq // n2
    n1f, n2f = FFT_INNER, seq // FFT_INNER

    tabs = {
        "fnet_outer": _outer_tables(n1f, n2f, False),
        "fnet_inner": _inner_table(n2f),
        "fnet_chan": _channel_table(g, 1.0 / math.sqrt(seq * g)),
        "hy_outer": _outer_tables(n1, n2, True),
        "filt_outer": _outer_tables(n1, n2, False),
        "inner": _inner_table(n2),
        "inv_inner": _inverse_inner_tables(n1, n2),
        "inv_outer": _inverse_outer_table(n1, n2),
    }
    tabs = {name: jnp.asarray(table) for name, table in tabs.items()}
    feats = _filter_features(positions, seq, n1, n2)
    mem_n = rmsnorm_bf16(mem.reshape(-1, d), mem_norm_g, min(512, mem.shape[0] * mem.shape[1]))

    x2 = x.reshape(bsz * seq, d)
    for l in range(depth):
        p = {
            "pre_g": pre_norm_g[l], "post_g": post_norm_g[l], "w_in": w_in[l],
            "conv_dw_w": conv_dw_w[l], "conv_dw_b": conv_dw_b[l], "conv_ln_g": conv_ln_g[l],
            "conv_ln_b": conv_ln_b[l], "conv_pw_w": conv_pw_w[l], "conv_pw_b": conv_pw_b[l],
            "fnet_w": fnet_w[l], "fnet_b": fnet_b[l], "hy_short_w": hy_short_w[l], "hy_short_b": hy_short_b[l],
            "hy_fw1": hy_fw1[l], "hy_fb1": hy_fb1[l], "hy_freq1": hy_freq1[l], "hy_fw2": hy_fw2[l],
            "hy_fb2": hy_fb2[l], "hy_freq2": hy_freq2[l], "hy_fw3": hy_fw3[l], "hy_decay": hy_decay[l],
            "hy_skip": hy_skip[l], "mem_wk": mem_wk[l], "mem_wv": mem_wv[l], "group_g": group_norm_g[l],
            "w_out": w_out[l],
        }
        x2 = _layer(x2, mem_n, feats, tabs, p, (bsz, seq, d, g))
    return x2.reshape(bsz, seq, d)
```

```python
import functools
import math

import jax
import jax.numpy as jnp
import numpy as np
from jax import lax
from jax.experimental import pallas as pl
from jax.experimental.pallas import tpu as pltpu

BF = jnp.bfloat16
F32 = jnp.float32
EPS = 1e-6

CONV_WIDTH = 31
CONV_HALF = CONV_WIDTH // 2
HALO = 16
SUBLANES, LANES = 8, 128
HYENA_BANDS = 16
HYENA_ORDER = 2
MEM_HEADS = 4
FFT_INNER = 128
FEAT_PAD = 128
FWD_COL = 2 * HYENA_BANDS + 1
BWD_COL = 2 * HYENA_BANDS + 2
DFT_RESIDUES = 16
VMEM_LIMIT = 56 * 1024 * 1024


def _params(*sem):
    return pltpu.CompilerParams(dimension_semantics=sem, vmem_limit_bytes=VMEM_LIMIT)


def _rmsnorm_kernel(x_ref, g_ref, o_ref):
    x = x_ref[...]
    ms = jnp.mean(x * x, axis=-1, keepdims=True)
    o_ref[...] = (x * lax.rsqrt(ms + EPS) * g_ref[...]).astype(o_ref.dtype)


def rmsnorm_bf16(x, g, tm):
    m, d = x.shape
    return pl.pallas_call(
        _rmsnorm_kernel,
        grid=(m // tm,),
        in_specs=[pl.BlockSpec((tm, d), lambda i: (i, 0)),
                  pl.BlockSpec((1, d), lambda i: (0, 0))],
        out_specs=pl.BlockSpec((tm, d), lambda i: (i, 0)),
        out_shape=jax.ShapeDtypeStruct((m, d), BF),
        compiler_params=_params("parallel"),
        name="rmsnorm",
    )(x, g.reshape(1, d))


def _mm_kernel(a_ref, w_ref, o_ref):
    o_ref[...] = jnp.dot(a_ref[...], w_ref[...], preferred_element_type=F32).astype(o_ref.dtype)


def matmul(a, w, tm, tn, name, col0=0, n=None):
    m, k = a.shape
    n = w.shape[1] if n is None else n
    tm, tn = min(tm, m), min(tn, n)
    joff = col0 // tn
    return pl.pallas_call(
        _mm_kernel,
        grid=(m // tm, n // tn),
        in_specs=[pl.BlockSpec((tm, k), lambda i, j: (i, 0)),
                  pl.BlockSpec((k, tn), lambda i, j: (0, j + joff))],
        out_specs=pl.BlockSpec((tm, tn), lambda i, j: (i, j)),
        out_shape=jax.ShapeDtypeStruct((m, n), BF),
        compiler_params=_params("parallel", "arbitrary"),
        name=name,
    )(a, w)


def _proj_kernel(x_ref, g_ref, w_ref, o_ref, h_s):
    @pl.when(pl.program_id(1) == 0)
    def _():
        x = x_ref[...]
        ms = jnp.mean(x * x, axis=-1, keepdims=True)
        h_s[...] = (x * lax.rsqrt(ms + EPS) * g_ref[...]).astype(h_s.dtype)

    o_ref[...] = jnp.dot(h_s[...], w_ref[...], preferred_element_type=F32).astype(o_ref.dtype)


def input_projection(x, g, w, tm, tn):
    m, d = x.shape
    n = w.shape[1]
    return pl.pallas_call(
        _proj_kernel,
        grid=(m // tm, n // tn),
        in_specs=[pl.BlockSpec((tm, d), lambda i, j: (i, 0)),
                  pl.BlockSpec((1, d), lambda i, j: (0, 0)),
                  pl.BlockSpec((d, tn), lambda i, j: (0, j))],
        out_specs=pl.BlockSpec((tm, tn), lambda i, j: (i, j)),
        out_shape=jax.ShapeDtypeStruct((m, n), BF),
        scratch_shapes=[pltpu.VMEM((tm, d), BF)],
        compiler_params=_params("parallel", "arbitrary"),
        name="input_projection",
    )(x, g.reshape(1, d), w)


def _halo_specs(tr, width, col):
    per = tr // HALO

    def cur(i):
        return (i, col)

    def prev(i):
        return (jnp.maximum(i * per - 1, 0), col)

    def make_next(nblocks):
        def nxt(i):
            return (jnp.minimum((i + 1) * per, nblocks - 1), col)
        return nxt

    return cur, prev, make_next


def _seq_edges(i, tr, seq):
    tiles = seq // tr
    pos = i % tiles
    return pos == 0, pos == tiles - 1


def _conva_kernel(vc, gc, vp, gp, vn, gn, dww, dwb, lng, lnb, pww, pwb, o_ref, upad, cbuf, wbc, *, seq):
    tr, c = vc.shape
    first, last = _seq_edges(pl.program_id(0), tr, seq)

    def glu(v, g):
        return v[...].astype(F32) * jax.nn.sigmoid(g[...].astype(F32))

    upad[pl.ds(HALO, tr), :] = glu(vc, gc)
    upad[pl.ds(0, HALO), :] = jnp.where(first, 0.0, glu(vp, gp))
    upad[pl.ds(HALO + tr, HALO), :] = jnp.where(last, 0.0, glu(vn, gn))

    for j in range(CONV_WIDTH):
        wbc[pl.ds(SUBLANES * j, SUBLANES), :] = jnp.broadcast_to(dww[pl.ds(j, 1), :], (SUBLANES, c))
    row = lax.broadcasted_iota(jnp.int32, (SUBLANES, LANES), 0)
    for ct in range(c // LANES):
        lanes = pl.ds(ct * LANES, LANES)

        def residue_sums(b, lanes=lanes):
            u = [upad[pl.ds(pl.multiple_of((b + a) * SUBLANES, SUBLANES), SUBLANES), lanes] for a in range(4)]
            sums = []
            for s in range(SUBLANES):
                acc = None
                for a in range(4):
                    j = SUBLANES * a + s - 1
                    if 0 <= j < CONV_WIDTH:
                        term = u[a] * wbc[pl.ds(SUBLANES * j, SUBLANES), lanes]
                        acc = term if acc is None else acc + term
                sums.append(acc)
            return tuple(sums)

        bias = jnp.broadcast_to(dwb[:, lanes], (SUBLANES, LANES))

        def body(b, prev, lanes=lanes, bias=bias, residue_sums=residue_sums):
            cur = residue_sums(b)
            terms = [bias + prev[0]]
            for s in range(1, SUBLANES):
                terms.append(pltpu.roll(jnp.where(row >= s, prev[s], cur[s]), SUBLANES - s, axis=0))
            while len(terms) > 1:
                terms = [terms[i] + terms[i + 1] for i in range(0, len(terms), 2)]
            out = terms[0]
            cbuf[pl.ds(pl.multiple_of((b - 1) * SUBLANES, SUBLANES), SUBLANES), lanes] = out
            return cur

        lax.fori_loop(1, tr // SUBLANES + 1, body, residue_sums(0), unroll=4)

    u = cbuf[...]
    mu = jnp.mean(u, axis=-1, keepdims=True)
    d = u - mu
    var = jnp.mean(d * d, axis=-1, keepdims=True)
    y = d * lax.rsqrt(var + EPS) * lng[...] + lnb[...]
    y = y * jax.nn.sigmoid(y)
    out = jnp.dot(y.astype(BF), pww[...], preferred_element_type=F32) + pwb[...]
    o_ref[...] = out.astype(o_ref.dtype)


def conv_module(a, dww, dwb, lng, lnb, pww, pwb, seq, tr):
    m = a.shape[0]
    c = pww.shape[0]
    cur, prev, make_next = _halo_specs(tr, c, 0)
    cur_g, prev_g, make_next_g = _halo_specs(tr, c, 1)
    nh = m // HALO
    row = lambda i: (0, 0)
    return pl.pallas_call(
        functools.partial(_conva_kernel, seq=seq),
        grid=(m // tr,),
        in_specs=[pl.BlockSpec((tr, c), cur), pl.BlockSpec((tr, c), cur_g),
                  pl.BlockSpec((HALO, c), prev), pl.BlockSpec((HALO, c), prev_g),
                  pl.BlockSpec((HALO, c), make_next(nh)), pl.BlockSpec((HALO, c), make_next_g(nh)),
                  pl.BlockSpec((CONV_WIDTH, c), row), pl.BlockSpec((1, c), row),
                  pl.BlockSpec((1, c), row), pl.BlockSpec((1, c), row),
                  pl.BlockSpec((c, c), row), pl.BlockSpec((1, c), row)],
        out_specs=pl.BlockSpec((tr, c), lambda i: (i, 0)),
        out_shape=jax.ShapeDtypeStruct((m, c), BF),
        scratch_shapes=[pltpu.VMEM((tr + 2 * HALO, c), F32), pltpu.VMEM((tr, c), F32),
                        pltpu.VMEM((CONV_WIDTH * SUBLANES, c), F32)],
        compiler_params=_params("parallel"),
        name="conv_module",
    )(a, a, a, a, a, a, dww, dwb.reshape(1, c), lng.reshape(1, c), lnb.reshape(1, c),
      pww.astype(BF), pwb.reshape(1, c))


def _short_kernel(hc, hp, hn, w, b, v_ref, x1_ref, x2_ref, *, seq):
    tr, c3 = hc.shape
    c = c3 // 3
    first, last = _seq_edges(pl.program_id(0), tr, seq)
    edge = lax.broadcasted_iota(jnp.int32, (HALO, c), 0)
    for g, o_ref in enumerate((v_ref, x1_ref, x2_ref)):
        cols = pl.ds(g * c, c)
        w0, w1, w2, bias = w[pl.ds(0, 1), cols], w[pl.ds(1, 1), cols], w[pl.ds(2, 1), cols], b[:, cols]
        cur = hc[:, cols].astype(F32)
        before = pltpu.roll(cur, 1, axis=0)
        after = pltpu.roll(cur, tr - 1, axis=0)
        o_ref[...] = (before * w0 + cur * w1 + after * w2 + bias).astype(o_ref.dtype)

        prev_row = jnp.where(first, 0.0, hp[pl.ds(HALO - 1, 1), cols].astype(F32))
        next_row = jnp.where(last, 0.0, hn[pl.ds(0, 1), cols].astype(F32))
        head = jnp.where(edge == 0, prev_row, before[:HALO])
        tail = jnp.where(edge == HALO - 1, next_row, after[tr - HALO:])
        o_ref[pl.ds(0, HALO), :] = (head * w0 + cur[:HALO] * w1 + after[:HALO] * w2 + bias).astype(o_ref.dtype)
        o_ref[pl.ds(tr - HALO, HALO), :] = (before[tr - HALO:] * w0 + cur[tr - HALO:] * w1 + tail * w2
                                            + bias).astype(o_ref.dtype)


def hyena_short_conv(hy, w, b, seq, tr, col):
    m = hy.shape[0]
    c3 = w.shape[1]
    c = c3 // 3
    cur, prev, make_next = _halo_specs(tr, c3, col)
    out = jax.ShapeDtypeStruct((m, c), BF)
    ospec = pl.BlockSpec((tr, c), lambda i: (i, 0))
    return pl.pallas_call(
        functools.partial(_short_kernel, seq=seq),
        grid=(m // tr,),
        in_specs=[pl.BlockSpec((tr, c3), cur), pl.BlockSpec((HALO, c3), prev),
                  pl.BlockSpec((HALO, c3), make_next(m // HALO)),
                  pl.BlockSpec((3, c3), lambda i: (0, 0)), pl.BlockSpec((1, c3), lambda i: (0, 0))],
        out_specs=[ospec, ospec, ospec],
        out_shape=[out, out, out],
        compiler_params=_params("parallel"),
        name="hyena_short_conv",
    )(hy, hy, hy, w, b.reshape(1, c3))


def _cis(p, n, sign):
    ang = (2.0 * np.pi / n) * (p % n)
    return np.cos(ang), sign * np.sin(ang)


def _stack_complex(mr, mi):
    return np.concatenate([np.concatenate([mr, -mi], axis=-1),
                           np.concatenate([mi, mr], axis=-1)], axis=-2)


def _bf16_const(a):
    return a.astype(jnp.bfloat16)


def _outer_stage_kernel(x_ref, t_ref, or_ref, oi_ref, yr_s, yi_s, *, nb):
    xt = pltpu.einshape("nrc->rnc", x_ref[...])
    for r in range(nb):
        y = jnp.dot(t_ref[r], xt[r], preferred_element_type=F32)
        h = y.shape[0] // 2
        yr_s[r] = y[:h].astype(BF)
        yi_s[r] = y[h:].astype(BF)
    or_ref[...] = pltpu.einshape("rnc->nrc", yr_s[...])
    oi_ref[...] = pltpu.einshape("rnc->nrc", yi_s[...])


def dft_outer_stage(x, table, nb, tc, name, c=None, col0=0):
    bsz, rows, n_inner, _ = x.shape
    c = x.shape[3] if c is None else c
    _, two_k, _ = table.shape
    kout = two_k // 2
    tc = min(tc, c)
    coff = col0 // tc
    out = jax.ShapeDtypeStruct((bsz, kout, n_inner, c), BF)
    ospec = pl.BlockSpec((None, kout, nb, tc), lambda b, j, cc: (b, 0, j, cc))
    return pl.pallas_call(
        functools.partial(_outer_stage_kernel, nb=nb),
        grid=(bsz, n_inner // nb, c // tc),
        in_specs=[pl.BlockSpec((None, rows, nb, tc), lambda b, j, cc: (b, 0, j, cc + coff)),
                  pl.BlockSpec((nb, two_k, rows), lambda b, j, cc: (j, 0, 0))],
        out_specs=[ospec, ospec],
        out_shape=[out, out],
        scratch_shapes=[pltpu.VMEM((nb, kout, tc), BF), pltpu.VMEM((nb, kout, tc), BF)],
        compiler_params=_params("parallel", "arbitrary", "arbitrary"),
        name=name,
    )(x, table)


def _hyena_mid_kernel(ur, ui, fur, fui, ss, d, minv, qr_ref, qi_ref, *, kb, n2):
    scale = lax.rsqrt(ss[...] + EPS)
    furt = pltpu.einshape("nkc->knc", fur[...])
    fuit = pltpu.einshape("nkc->knc", fui[...])
    for k in range(kb):
        rows = pl.ds(k * n2, n2)
        x = jnp.dot(d[...], jnp.concatenate([ur[rows, :], ui[rows, :]], axis=0), preferred_element_type=F32)
        f = jnp.dot(d[...], jnp.concatenate([furt[k], fuit[k]], axis=0), preferred_element_type=F32) * scale
        xr, xi = x[:n2], x[n2:]
        fr, fi = f[:n2], f[n2:]
        pr = xr * fr - xi * fi
        pi = xr * fi + xi * fr
        catp = jnp.concatenate([pr, pi], axis=0).astype(BF)
        q = jnp.dot(minv[k], catp, preferred_element_type=F32)
        qr_ref[rows, :] = q[:n2].astype(qr_ref.dtype)
        qi_ref[rows, :] = q[n2:].astype(qi_ref.dtype)


def hyena_mid_stage(ur, ui, fur, fui, ss, d, minv, order, kb, tc):
    n, c = ur.shape
    n2, n1, _ = fur.shape
    tc = min(tc, c)
    coff = order * (c // tc)
    spec = pl.BlockSpec((kb * n2, tc), lambda i, cc: (i, cc))
    fspec = pl.BlockSpec((n2, kb, tc), lambda i, cc: (0, i, cc + coff))
    out = jax.ShapeDtypeStruct((n, c), BF)
    return pl.pallas_call(
        functools.partial(_hyena_mid_kernel, kb=kb, n2=n2),
        grid=(n1 // kb, c // tc),
        in_specs=[spec, spec, fspec, fspec,
                  pl.BlockSpec((1, tc), lambda i, cc: (0, cc + coff)),
                  pl.BlockSpec((2 * n2, 2 * n2), lambda i, cc: (0, 0)),
                  pl.BlockSpec((kb, 2 * n2, 2 * n2), lambda i, cc: (i, 0, 0))],
        out_specs=[spec, spec],
        out_shape=[out, out],
        compiler_params=_params("parallel", "arbitrary"),
        name="hyena_mid",
    )(ur, ui, fur, fui, ss, d, minv)


def _hyena_out_kernel(qr, qi, g, xo, z, skip, o_ref, y_s, *, nb):
    qrt = pltpu.einshape("nrc->rnc", qr[...])
    qit = pltpu.einshape("nrc->rnc", qi[...])
    for r in range(nb):
        cat = jnp.concatenate([qrt[r], qit[r]], axis=0)
        y_s[r] = jnp.dot(g[...], cat, preferred_element_type=F32)
    y = pltpu.einshape("rnc->nrc", y_s[...])
    zf = z[...].astype(F32)
    o_ref[...] = (xo[...].astype(F32) * (y + skip[...] * zf)).astype(o_ref.dtype)


def hyena_out_stage(qr, qi, g, xo, z, skip, nb, tc):
    rows, n2, c = qr.shape
    tc = min(tc, c)
    spec = pl.BlockSpec((rows, nb, tc), lambda j, cc: (0, j, cc))
    return pl.pallas_call(
        functools.partial(_hyena_out_kernel, nb=nb),
        grid=(n2 // nb, c // tc),
        in_specs=[spec, spec, pl.BlockSpec((rows, 2 * rows), lambda j, cc: (0, 0)), spec, spec,
                  pl.BlockSpec((1, tc), lambda j, cc: (0, cc))],
        out_specs=spec,
        out_shape=jax.ShapeDtypeStruct((rows, n2, c), BF),
        scratch_shapes=[pltpu.VMEM((nb, rows, tc), F32)],
        compiler_params=_params("parallel", "arbitrary"),
        name="hyena_out",
    )(qr, qi, g, xo, z, skip)


def _dot_bf16x3(a, b):
    ah = a.astype(BF)
    bh = b.astype(BF)
    al = (a - ah.astype(F32)).astype(BF)
    bl = (b - bh.astype(F32)).astype(BF)
    return jnp.dot(jnp.concatenate([ah, al, ah], axis=1), jnp.concatenate([bh, bh, bl], axis=0),
                   preferred_element_type=F32)


def _filter_kernel(ft, w1, b1, f1, w2, b2, f2, w3, decf, decb, tab, ur_ref, ui_ref, ss_ref, *, n1):
    hi = lax.Precision.HIGHEST
    x = ft[...]
    ffn2 = w2.shape[1]
    h = jnp.sin(f1[...] * (jnp.dot(x, w1[...], precision=hi, preferred_element_type=F32) + b1[...]))
    h = jnp.sin(f2[...] * (jnp.dot(h, w2[...], precision=hi, preferred_element_type=F32) + b2[...]))
    t = x[:, 0:1]
    is_f = x[:, FWD_COL:FWD_COL + 1]
    is_b = x[:, BWD_COL:BWD_COL + 1]
    lane = lax.broadcasted_iota(jnp.int32, h.shape, 1)
    h = h * jnp.where(lane < ffn2 // 2, is_f, is_b)
    h = _dot_bf16x3(h, w3[...])
    dec = is_f * jnp.abs(decf[...]) + is_b * jnp.abs(decb[...])
    k = h * jnp.exp(-t * dec)
    kb = k.astype(BF)
    for r in range(tab.shape[0]):
        rows = pl.ds(r * n1, n1)
        u = jnp.dot(tab[r], kb[r * n1:(r + 1) * n1], preferred_element_type=F32)
        ur_ref[rows, :] = u[:n1].astype(ur_ref.dtype)
        ui_ref[rows, :] = u[n1:].astype(ui_ref.dtype)
    part = jnp.sum(k * k, axis=0, keepdims=True)

    @pl.when(pl.program_id(0) == 0)
    def _():
        ss_ref[...] = jnp.zeros_like(ss_ref)

    ss_ref[...] += jnp.broadcast_to(part, ss_ref.shape)


def hyena_filter_taps(feats, w1, b1, f1, w2, b2, f2, w3, decf, decb, table, nb):
    n, fp = feats.shape
    ffn = w1.shape[1]
    ffn2 = w2.shape[1]
    oc = w3.shape[-1]
    _, two_n1, n1 = table.shape
    tr = nb * n1
    small = lambda shape: pl.BlockSpec(shape, lambda i: (0,) * len(shape))
    ospec = pl.BlockSpec((tr, oc), lambda i: (i, 0))
    out = jax.ShapeDtypeStruct((n, oc), BF)
    return pl.pallas_call(
        functools.partial(_filter_kernel, n1=n1),
        grid=(n // tr,),
        in_specs=[pl.BlockSpec((tr, fp), lambda i: (i, 0)),
                  small((fp, ffn)), small((1, ffn)), small((1, ffn)),
                  small((ffn, ffn2)), small((1, ffn2)), small((1, ffn2)),
                  small((ffn2, oc)), small((1, oc)), small((1, oc)),
                  pl.BlockSpec((nb, two_n1, n1), lambda i: (i, 0, 0))],
        out_specs=[ospec, ospec, pl.BlockSpec((8, oc), lambda i: (0, 0))],
        out_shape=[out, out, jax.ShapeDtypeStruct((8, oc), F32)],
        compiler_params=_params("arbitrary"),
        name="hyena_filter_taps",
    )(feats, w1, b1, f1, w2, b2, f2, w3, decf, decb, table)


def _fnet_kernel(yr, yi, d2, csgw, b, o_ref, xcat, o_s, *, kb, n2, c):
    for k in range(kb):
        rows = pl.ds(k * n2, n2)
        cat = jnp.concatenate([yr[rows, :], yi[rows, :]], axis=0)
        x = jnp.dot(d2[...], cat, preferred_element_type=F32)
        xcat[rows, pl.ds(0, c)] = x[:n2].astype(BF)
        xcat[rows, pl.ds(c, c)] = x[n2:].astype(BF)
    out = jnp.dot(xcat[...], csgw[...], preferred_element_type=F32) + b[...]
    for k in range(kb):
        o_s[k] = out[k * n2:(k + 1) * n2].astype(o_s.dtype)
    o_ref[...] = pltpu.einshape("kmc->mkc", o_s[...])


def fnet_inner_stage(yr, yi, d2, csgw, b, n2, kb):
    bsz, l, c = yr.shape
    n1 = l // n2
    kb = min(kb, n1)
    spec = pl.BlockSpec((None, kb * n2, c), lambda bb, i: (bb, i, 0))
    const = lambda shape: pl.BlockSpec(shape, lambda bb, i: (0,) * len(shape))
    return pl.pallas_call(
        functools.partial(_fnet_kernel, kb=kb, n2=n2, c=c),
        grid=(bsz, n1 // kb),
        in_specs=[spec, spec, const((2 * n2, 2 * n2)), const((2 * c, c)), const((1, c))],
        out_specs=pl.BlockSpec((None, n2, kb, c), lambda bb, i: (bb, 0, i, 0)),
        out_shape=jax.ShapeDtypeStruct((bsz, n2, n1, c), BF),
        scratch_shapes=[pltpu.VMEM((kb * n2, 2 * c), BF), pltpu.VMEM((kb, n2, c), BF)],
        compiler_params=_params("parallel", "arbitrary"),
        name="fnet_inner",
    )(yr, yi, d2, csgw, b)


def _attn_kernel(q_ref, k_ref, v_ref, o_ref, *, heads):
    c = q_ref.shape[1]
    hd = c // heads
    scale = hd ** -0.5
    for h in range(heads):
        cols = pl.ds(h * hd, hd)
        s = lax.dot_general(q_ref[:, cols], k_ref[:, cols], (((1,), (1,)), ((), ())),
                            preferred_element_type=F32) * scale
        s = s - jnp.max(s, axis=-1, keepdims=True)
        e = jnp.exp(s)
        p = e / jnp.sum(e, axis=-1, keepdims=True)
        o = jnp.dot(p.astype(BF), v_ref[:, cols], preferred_element_type=F32)
        o_ref[:, cols] = o.astype(o_ref.dtype)


def memory_attention(q, kh, vh, seq, tr, col):
    m = q.shape[0]
    _, nm, c = kh.shape
    per = seq // tr
    kv = pl.BlockSpec((None, nm, c), lambda i: (i // per, 0, 0))
    return pl.pallas_call(
        functools.partial(_attn_kernel, heads=MEM_HEADS),
        grid=(m // tr,),
        in_specs=[pl.BlockSpec((tr, c), lambda i: (i, col)), kv, kv],
        out_specs=pl.BlockSpec((tr, c), lambda i: (i, 0)),
        out_shape=jax.ShapeDtypeStruct((m, c), BF),
        compiler_params=_params("parallel"),
        name="memory_attention",
    )(q, kh, vh)


def _merge_kernel(ya, yb, yc, ym, ga, gb, gc, gm, gg, w, x, pg, o_ref):
    g = ya.shape[1]
    acc = None
    for idx, (y_ref, gate) in enumerate(((ya, ga), (yb, gb), (yc, gc), (ym, gm))):
        cols = pl.ds(idx * g, g)
        y = y_ref[...].astype(F32)
        ms = jnp.mean(y * y, axis=-1, keepdims=True)
        yn = y * lax.rsqrt(ms + EPS) * gg[:, cols]
        gt = gate[...]
        yn = yn.astype(BF) * (gt * jax.nn.sigmoid(gt))
        part = jnp.dot(yn, w[cols, :], preferred_element_type=F32)
        acc = part if acc is None else acc + part
    ms = jnp.mean(acc * acc, axis=-1, keepdims=True)
    o_ref[...] = x[...] + acc * lax.rsqrt(ms + EPS) * pg[...]


def merge_project(ya, yb, yc, ym, gate, gate_col, gg, w_out, x, pg, tm):
    m, g = ya.shape
    d = x.shape[1]
    yspec = pl.BlockSpec((tm, g), lambda i: (i, 0))
    gspecs = [pl.BlockSpec((tm, g), lambda i, col=gate_col + k: (i, col)) for k in range(4)]
    const = lambda shape: pl.BlockSpec(shape, lambda i: (0, 0))
    return pl.pallas_call(
        _merge_kernel,
        grid=(m // tm,),
        in_specs=[yspec, yspec, yspec, yspec, *gspecs,
                  const((1, 4 * g)),
                  pl.BlockSpec((4 * g, d), lambda i: (0, 0), pipeline_mode=pl.Buffered(1)),
                  pl.BlockSpec((tm, d), lambda i: (i, 0)),
                  const((1, d))],
        out_specs=pl.BlockSpec((tm, d), lambda i: (i, 0)),
        out_shape=jax.ShapeDtypeStruct((m, d), F32),
        compiler_params=_params("parallel"),
        name="merge_project",
    )(ya, yb, yc, ym, gate, gate, gate, gate, gg, w_out, x, pg)


def _iota(n):
    return np.arange(n, dtype=np.int64)


@functools.lru_cache(maxsize=None)
def _outer_tables(n1, n2, rows_complex):
    n = n1 * n2
    r = _iota(n2)[:, None, None]
    k1 = _iota(n1)[None, :, None]
    nj = n1 // 2 if rows_complex else n1
    j = _iota(nj)[None, None, :]
    er, ei = _cis(k1 * (n2 * j + r), n, -1.0)
    if rows_complex:
        return _bf16_const(_stack_complex(er, ei))
    return _bf16_const(np.concatenate([er, ei], axis=-2))


@functools.lru_cache(maxsize=None)
def _inner_table(n2):
    k = _iota(n2)[:, None]
    j = _iota(n2)[None, :]
    cr, ci = _cis(k * j, n2, -1.0)
    return _bf16_const(_stack_complex(cr, ci))


@functools.lru_cache(maxsize=None)
def _inverse_inner_tables(n1, n2):
    n = n1 * n2
    k1 = _iota(n1)[:, None, None]
    na = _iota(n2)[None, :, None]
    k2 = _iota(n2)[None, None, :]
    ar, ai = _cis(na * (k1 + n1 * k2), n, 1.0)
    return _bf16_const(_stack_complex(ar, ai))


@functools.lru_cache(maxsize=None)
def _inverse_outer_table(n1, n2):
    nb = _iota(n1 // 2)[:, None]
    k1 = _iota(n1)[None, :]
    gr, gi = _cis(nb * k1, n1, 1.0)
    inv = 1.0 / (n1 * n2)
    return _bf16_const(_stack_complex(gr * inv, gi * inv))


@functools.lru_cache(maxsize=None)
def _channel_table(g, scale):
    a = _iota(g)[:, None]
    b = _iota(g)[None, :]
    cg, sg = _cis(a * b, g, 1.0)
    return _bf16_const(np.concatenate([cg, sg], axis=0) * scale)


def _filter_features(positions, seq, n1, n2):
    idx = (n2 * _iota(n1)[None, :] + _iota(n2)[:, None]).reshape(-1)
    pos = positions.astype(F32)
    pos = jnp.concatenate([pos, pos[:1], jnp.flip(pos[1:])])
    pos = pos.reshape(n1, n2).T.reshape(-1)
    t = pos / seq
    bands = jnp.linspace(1e-4, HYENA_BANDS - 1, HYENA_BANDS, dtype=F32)
    w = 2.0 * jnp.pi * pos / seq
    is_f = (idx < seq).astype(F32)
    is_b = (idx > seq).astype(F32)
    feats = jnp.concatenate([t[:, None], jnp.cos(w[:, None] * bands), jnp.sin(w[:, None] * bands),
                             is_f[:, None], is_b[:, None]], axis=-1)
    return jnp.pad(feats, ((0, 0), (0, FEAT_PAD - feats.shape[1])))


def _layer(x2, mem_n, feats, tabs, p, dims):
    bsz, seq, d, g = dims
    m = bsz * seq
    n2 = FFT_INNER
    n1 = 2 * seq // n2
    n1f, n2f = FFT_INNER, seq // FFT_INNER

    proj = input_projection(x2, p["pre_g"], p["w_in"].astype(BF), 1024, g)
    n_in = proj.shape[1]

    y_a = conv_module(proj, p["conv_dw_w"], p["conv_dw_b"], p["conv_ln_g"], p["conv_ln_b"],
                      p["conv_pw_w"], p["conv_pw_b"], seq, 512)

    yr, yi = dft_outer_stage(proj.reshape(bsz, n1f, n2f, n_in), tabs["fnet_outer"], DFT_RESIDUES, 512,
                             "fnet_outer", c=g, col0=2 * g)
    csgw = matmul(tabs["fnet_chan"], p["fnet_w"].astype(BF), 1024, 1024, "fnet_fold")
    y_b = fnet_inner_stage(yr.reshape(bsz, seq, g), yi.reshape(bsz, seq, g), tabs["fnet_inner"],
                           csgw, p["fnet_b"].reshape(1, g), n2f, 16)
    y_b = y_b.reshape(m, g)

    v, x1, x2g = hyena_short_conv(proj, p["hy_short_w"], p["hy_short_b"], seq, 512, 1)
    ffn = p["hy_fw1"].shape[1]
    og = HYENA_ORDER * g
    w1 = jnp.pad(p["hy_fw1"], ((0, FEAT_PAD - p["hy_fw1"].shape[0]), (0, 0)))
    dup = lambda a: jnp.concatenate([a, a], axis=-1)
    w3 = p["hy_fw3"].reshape(ffn, HYENA_ORDER, 2, g)
    w3 = jnp.concatenate([w3[:, :, 0, :].reshape(ffn, og), w3[:, :, 1, :].reshape(ffn, og)], axis=0)
    kur, kui, ss = hyena_filter_taps(
        feats, w1, p["hy_fb1"].reshape(1, ffn), p["hy_freq1"].reshape(1, ffn),
        dup(p["hy_fw2"]), dup(p["hy_fb2"].reshape(1, ffn)), dup(p["hy_freq2"].reshape(1, ffn)),
        w3, p["hy_decay"][:, 0, :].reshape(1, og), p["hy_decay"][:, 1, :].reshape(1, og),
        tabs["filt_outer"], max(1, 512 // n1))
    kur, kui = kur.reshape(n2, n1, og), kui.reshape(n2, n1, og)

    z = v
    for o, gate_o in enumerate((x1, x2g)):
        ur, ui = dft_outer_stage(z.reshape(1, n1, n2, g), tabs["hy_outer"], DFT_RESIDUES, 512, "hyena_outer_dft")
        qr, qi = hyena_mid_stage(ur.reshape(n1 * n2, g), ui.reshape(n1 * n2, g), kur, kui, ss[0:1],
                                 tabs["inner"], tabs["inv_inner"], o, DFT_RESIDUES, 512)
        z = hyena_out_stage(qr.reshape(n1, n2, g), qi.reshape(n1, n2, g), tabs["inv_outer"],
                            gate_o.reshape(n1, n2, g), z.reshape(n1, n2, g), p["hy_skip"][o].reshape(1, g),
                            DFT_RESIDUES, 512)
    y_c = z.reshape(m, g)

    nm = mem_n.shape[0] // bsz
    kh = matmul(mem_n, p["mem_wk"].astype(BF), 512, 1024, "mem_k").reshape(bsz, nm, g)
    vh = matmul(mem_n, p["mem_wv"].astype(BF), 512, 1024, "mem_v").reshape(bsz, nm, g)
    y_m = memory_attention(proj, kh, vh, seq, 1024, 6)

    return merge_project(y_a, y_b, y_c, y_m, proj, 7, p["group_g"].reshape(1, 4 * g), p["w_out"].astype(BF),
                         x2, p["post_g"].reshape(1, d), 256)


def kernel(x, mem, positions, mem_norm_g, pre_norm_g, post_norm_g, w_in, conv_dw_w, conv_dw_b, conv_ln_g, conv_ln_b, conv_pw_w, conv_pw_b, fnet_w, fnet_b, hy_short_w, hy_short_b, hy_fw1, hy_fb1, hy_freq1, hy_fw2, hy_fb2, hy_freq2, hy_fw3, hy_decay, hy_skip, mem_wk, mem_wv, group_norm_g, w_out):
    bsz, seq, d = x.shape
    g = conv_pw_w.shape[-1]
    depth = w_in.shape[0]
    assert bsz == 2, "the long convolution packs exactly two batch elements as one complex sequence"
    assert seq % (FFT_INNER * DFT_RESIDUES) == 0
    n2 = FFT_INNER
    n1 = 2 * seq // n2
    n1f, n2f = FFT_INNER, seq // FFT_INNER

    tabs = {
        "fnet_outer": _outer_tables(n1f, n2f, False),
        "fnet_inner": _inner_table(n2f),
        "fnet_chan": _channel_table(g, 1.0 / math.sqrt(seq * g)),
        "hy_outer": _outer_tables(n1, n2, True),
        "filt_outer": _outer_tables(n1, n2, False),
        "inner": _inner_table(n2),
        "inv_inner": _inverse_inner_tables(n1, n2),
        "inv_outer": _inverse_outer_table(n1, n2),
    }
    tabs = {name: jnp.asarray(table) for name, table in tabs.items()}
    feats = _filter_features(positions, seq, n1, n2)
    mem_n = rmsnorm_bf16(mem.reshape(-1, d), mem_norm_g, min(512, mem.shape[0] * mem.shape[1]))

    x2 = x.reshape(bsz * seq, d)
    for l in range(depth):
        p = {
            "pre_g": pre_norm_g[l], "post_g": post_norm_g[l], "w_in": w_in[l],
            "conv_dw_w": conv_dw_w[l], "conv_dw_b": conv_dw_b[l], "conv_ln_g": conv_ln_g[l],
            "conv_ln_b": conv_ln_b[l], "conv_pw_w": conv_pw_w[l], "conv_pw_b": conv_pw_b[l],
            "fnet_w": fnet_w[l], "fnet_b": fnet_b[l], "hy_short_w": hy_short_w[l], "hy_short_b": hy_short_b[l],
            "hy_fw1": hy_fw1[l], "hy_fb1": hy_fb1[l], "hy_freq1": hy_freq1[l], "hy_fw2": hy_fw2[l],
            "hy_fb2": hy_fb2[l], "hy_freq2": hy_freq2[l], "hy_fw3": hy_fw3[l], "hy_decay": hy_decay[l],
            "hy_skip": hy_skip[l], "mem_wk": mem_wk[l], "mem_wv": mem_wv[l], "group_g": group_norm_g[l],
            "w_out": w_out[l],
        }
        x2 = _layer(x2, mem_n, feats, tabs, p, (bsz, seq, d, g))
    return x2.reshape(bsz, seq, d)
```

```python
import functools
import math

import jax
import jax.numpy as jnp
import numpy as np
from jax import lax
from jax.experimental import pallas as pl
from jax.experimental.pallas import tpu as pltpu

BF = jnp.bfloat16
F32 = jnp.float32
EPS = 1e-6

CONV_WIDTH = 31
CONV_HALF = CONV_WIDTH // 2
HALO = 16
SUBLANES, LANES = 8, 128
HYENA_BANDS = 16
HYENA_ORDER = 2
MEM_HEADS = 4
FFT_INNER = 128
FEAT_PAD = 128
FWD_COL = 2 * HYENA_BANDS + 1
BWD_COL = 2 * HYENA_BANDS + 2
DFT_RESIDUES = 16
VMEM_LIMIT = 56 * 1024 * 1024


def _params(*sem):
    return pltpu.CompilerParams(dimension_semantics=sem, vmem_limit_bytes=VMEM_LIMIT)


def _rmsnorm_kernel(x_ref, g_ref, o_ref):
    x = x_ref[...]
    ms = jnp.mean(x * x, axis=-1, keepdims=True)
    o_ref[...] = (x * lax.rsqrt(ms + EPS) * g_ref[...]).astype(o_ref.dtype)


def rmsnorm_bf16(x, g, tm):
    m, d = x.shape
    return pl.pallas_call(
        _rmsnorm_kernel,
        grid=(m // tm,),
        in_specs=[pl.BlockSpec((tm, d), lambda i: (i, 0)),
                  pl.BlockSpec((1, d), lambda i: (0, 0))],
        out_specs=pl.BlockSpec((tm, d), lambda i: (i, 0)),
        out_shape=jax.ShapeDtypeStruct((m, d), BF),
        compiler_params=_params("parallel"),
        name="rmsnorm",
    )(x, g.reshape(1, d))


def _mm_kernel(a_ref, w_ref, o_ref):
    o_ref[...] = jnp.dot(a_ref[...], w_ref[...], preferred_element_type=F32).astype(o_ref.dtype)


def matmul(a, w, tm, tn, name, col0=0, n=None):
    m, k = a.shape
    n = w.shape[1] if n is None else n
    tm, tn = min(tm, m), min(tn, n)
    joff = col0 // tn
    return pl.pallas_call(
        _mm_kernel,
        grid=(m // tm, n // tn),
        in_specs=[pl.BlockSpec((tm, k), lambda i, j: (i, 0)),
                  pl.BlockSpec((k, tn), lambda i, j: (0, j + joff))],
        out_specs=pl.BlockSpec((tm, tn), lambda i, j: (i, j)),
        out_shape=jax.ShapeDtypeStruct((m, n), BF),
        compiler_params=_params("parallel", "arbitrary"),
        name=name,
    )(a, w)


def _proj_kernel(x_ref, g_ref, w_ref, o_ref, h_s):
    @pl.when(pl.program_id(1) == 0)
    def _():
        x = x_ref[...]
        ms = jnp.mean(x * x, axis=-1, keepdims=True)
        h_s[...] = (x * lax.rsqrt(ms + EPS) * g_ref[...]).astype(h_s.dtype)

    o_ref[...] = jnp.dot(h_s[...], w_ref[...], preferred_element_type=F32).astype(o_ref.dtype)


def input_projection(x, g, w, tm, tn):
    m, d = x.shape
    n = w.shape[1]
    return pl.pallas_call(
        _proj_kernel,
        grid=(m // tm, n // tn),
        in_specs=[pl.BlockSpec((tm, d), lambda i, j: (i, 0)),
                  pl.BlockSpec((1, d), lambda i, j: (0, 0)),
                  pl.BlockSpec((d, tn), lambda i, j: (0, j))],
        out_specs=pl.BlockSpec((tm, tn), lambda i, j: (i, j)),
        out_shape=jax.ShapeDtypeStruct((m, n), BF),
        scratch_shapes=[pltpu.VMEM((tm, d), BF)],
        compiler_params=_params("parallel", "arbitrary"),
        name="input_projection",
    )(x, g.reshape(1, d), w)


def _halo_specs(tr, width, col):
    per = tr // HALO

    def cur(i):
        return (i, col)

    def prev(i):
        return (jnp.maximum(i * per - 1, 0), col)

    def make_next(nblocks):
        def nxt(i):
            return (jnp.minimum((i + 1) * per, nblocks - 1), col)
        return nxt

    return cur, prev, make_next


def _seq_edges(i, tr, seq):
    tiles = seq // tr
    pos = i % tiles
    return pos == 0, pos == tiles - 1


def _conva_kernel(vc, gc, vp, gp, vn, gn, dww, dwb, lng, lnb, pww, pwb, o_ref, upad, cbuf, wbc, *, seq):
    tr, c = vc.shape
    first, last = _seq_edges(pl.program_id(0), tr, seq)

    def glu(v, g):
        return v[...].astype(F32) * jax.nn.sigmoid(g[...].astype(F32))

    upad[pl.ds(HALO, tr), :] = glu(vc, gc)
    upad[pl.ds(0, HALO), :] = jnp.where(first, 0.0, glu(vp, gp))
    upad[pl.ds(HALO + tr, HALO), :] = jnp.where(last, 0.0, glu(vn, gn))

    for j in range(CONV_WIDTH):
        wbc[pl.ds(SUBLANES * j, SUBLANES), :] = jnp.broadcast_to(dww[pl.ds(j, 1), :], (SUBLANES, c))
    row = lax.broadcasted_iota(jnp.int32, (SUBLANES, LANES), 0)
    for ct in range(c // LANES):
        lanes = pl.ds(ct * LANES, LANES)

        def residue_sums(b, lanes=lanes):
            u = [upad[pl.ds(pl.multiple_of((b + a) * SUBLANES, SUBLANES), SUBLANES), lanes] for a in range(4)]
            sums = []
            for s in range(SUBLANES):
                acc = None
                for a in range(4):
                    j = SUBLANES * a + s - 1
                    if 0 <= j < CONV_WIDTH:
                        term = u[a] * wbc[pl.ds(SUBLANES * j, SUBLANES), lanes]
                        acc = term if acc is None else acc + term
                sums.append(acc)
            return tuple(sums)

        bias = jnp.broadcast_to(dwb[:, lanes], (SUBLANES, LANES))

        def body(b, prev, lanes=lanes, bias=bias, residue_sums=residue_sums):
            cur = residue_sums(b)
            terms = [bias + prev[0]]
            for s in range(1, SUBLANES):
                terms.append(pltpu.roll(jnp.where(row >= s, prev[s], cur[s]), SUBLANES - s, axis=0))
            while len(terms) > 1:
                terms = [terms[i] + terms[i + 1] for i in range(0, len(terms), 2)]
            out = terms[0]
            cbuf[pl.ds(pl.multiple_of((b - 1) * SUBLANES, SUBLANES), SUBLANES), lanes] = out
            return cur

        lax.fori_loop(1, tr // SUBLANES + 1, body, residue_sums(0), unroll=4)

    u = cbuf[...]
    mu = jnp.mean(u, axis=-1, keepdims=True)
    d = u - mu
    var = jnp.mean(d * d, axis=-1, keepdims=True)
    y = d * lax.rsqrt(var + EPS) * lng[...] + lnb[...]
    y = y * jax.nn.sigmoid(y)
    out = jnp.dot(y.astype(BF), pww[...], preferred_element_type=F32) + pwb[...]
    o_ref[...] = out.astype(o_ref.dtype)


def conv_module(a, dww, dwb, lng, lnb, pww, pwb, seq, tr):
    m = a.shape[0]
    c = pww.shape[0]
    cur, prev, make_next = _halo_specs(tr, c, 0)
    cur_g, prev_g, make_next_g = _halo_specs(tr, c, 1)
    nh = m // HALO
    row = lambda i: (0, 0)
    return pl.pallas_call(
        functools.partial(_conva_kernel, seq=seq),
        grid=(m // tr,),
        in_specs=[pl.BlockSpec((tr, c), cur), pl.BlockSpec((tr, c), cur_g),
                  pl.BlockSpec((HALO, c), prev), pl.BlockSpec((HALO, c), prev_g),
                  pl.BlockSpec((HALO, c), make_next(nh)), pl.BlockSpec((HALO, c), make_next_g(nh)),
                  pl.BlockSpec((CONV_WIDTH, c), row), pl.BlockSpec((1, c), row),
                  pl.BlockSpec((1, c), row), pl.BlockSpec((1, c), row),
                  pl.BlockSpec((c, c), row), pl.BlockSpec((1, c), row)],
        out_specs=pl.BlockSpec((tr, c), lambda i: (i, 0)),
        out_shape=jax.ShapeDtypeStruct((m, c), BF),
        scratch_shapes=[pltpu.VMEM((tr + 2 * HALO, c), F32), pltpu.VMEM((tr, c), F32),
                        pltpu.VMEM((CONV_WIDTH * SUBLANES, c), F32)],
        compiler_params=_params("parallel"),
        name="conv_module",
    )(a, a, a, a, a, a, dww, dwb.reshape(1, c), lng.reshape(1, c), lnb.reshape(1, c),
      pww.astype(BF), pwb.reshape(1, c))


def _short_kernel(hc, hp, hn, w, b, v_ref, x1_ref, x2_ref, *, seq):
    tr, c3 = hc.shape
    c = c3 // 3
    first, last = _seq_edges(pl.program_id(0), tr, seq)
    edge = lax.broadcasted_iota(jnp.int32, (HALO, c), 0)
    for g, o_ref in enumerate((v_ref, x1_ref, x2_ref)):
        cols = pl.ds(g * c, c)
        w0, w1, w2, bias = w[pl.ds(0, 1), cols], w[pl.ds(1, 1), cols], w[pl.ds(2, 1), cols], b[:, cols]
        cur = hc[:, cols].astype(F32)
        before = pltpu.roll(cur, 1, axis=0)
        after = pltpu.roll(cur, tr - 1, axis=0)
        o_ref[...] = (before * w0 + cur * w1 + after * w2 + bias).astype(o_ref.dtype)

        prev_row = jnp.where(first, 0.0, hp[pl.ds(HALO - 1, 1), cols].astype(F32))
        next_row = jnp.where(last, 0.0, hn[pl.ds(0, 1), cols].astype(F32))
        head = jnp.where(edge == 0, prev_row, before[:HALO])
        tail = jnp.where(edge == HALO - 1, next_row, after[tr - HALO:])
        o_ref[pl.ds(0, HALO), :] = (head * w0 + cur[:HALO] * w1 + after[:HALO] * w2 + bias).astype(o_ref.dtype)
        o_ref[pl.ds(tr - HALO, HALO), :] = (before[tr - HALO:] * w0 + cur[tr - HALO:] * w1 + tail * w2
                                            + bias).astype(o_ref.dtype)


def hyena_short_conv(hy, w, b, seq, tr, col):
    m = hy.shape[0]
    c3 = w.shape[1]
    c = c3 // 3
    cur, prev, make_next = _halo_specs(tr, c3, col)
    out = jax.ShapeDtypeStruct((m, c), BF)
    ospec = pl.BlockSpec((tr, c), lambda i: (i, 0))
    return pl.pallas_call(
        functools.partial(_short_kernel, seq=seq),
        grid=(m // tr,),
        in_specs=[pl.BlockSpec((tr, c3), cur), pl.BlockSpec((HALO, c3), prev),
                  pl.BlockSpec((HALO, c3), make_next(m // HALO)),
                  pl.BlockSpec((3, c3), lambda i: (0, 0)), pl.BlockSpec((1, c3), lambda i: (0, 0))],
        out_specs=[ospec, ospec, ospec],
        out_shape=[out, out, out],
        compiler_params=_params("parallel"),
        name="hyena_short_conv",
    )(hy, hy, hy, w, b.reshape(1, c3))


def _cis(p, n, sign):
    ang = (2.0 * np.pi / n) * (p % n)
    return np.cos(ang), sign * np.sin(ang)


def _stack_complex(mr, mi):
    return np.concatenate([np.concatenate([mr, -mi], axis=-1),
                           np.concatenate([mi, mr], axis=-1)], axis=-2)


def _bf16_const(a):
    return a.astype(jnp.bfloat16)


def _outer_stage_kernel(x_ref, t_ref, or_ref, oi_ref, yr_s, yi_s, *, nb):
    xt = pltpu.einshape("nrc->rnc", x_ref[...])
    for r in range(nb):
        y = jnp.dot(t_ref[r], xt[r], preferred_element_type=F32)
        h = y.shape[0] // 2
        yr_s[r] = y[:h].astype(BF)
        yi_s[r] = y[h:].astype(BF)
    or_ref[...] = pltpu.einshape("rnc->nrc", yr_s[...])
    oi_ref[...] = pltpu.einshape("rnc->nrc", yi_s[...])


def dft_outer_stage(x, table, nb, tc, name, c=None, col0=0):
    bsz, rows, n_inner, _ = x.shape
    c = x.shape[3] if c is None else c
    _, two_k, _ = table.shape
    kout = two_k // 2
    tc = min(tc, c)
    coff = col0 // tc
    out = jax.ShapeDtypeStruct((bsz, kout, n_inner, c), BF)
    ospec = pl.BlockSpec((None, kout, nb, tc), lambda b, j, cc: (b, 0, j, cc))
    return pl.pallas_call(
        functools.partial(_outer_stage_kernel, nb=nb),
        grid=(bsz, n_inner // nb, c // tc),
        in_specs=[pl.BlockSpec((None, rows, nb, tc), lambda b, j, cc: (b, 0, j, cc + coff)),
                  pl.BlockSpec((nb, two_k, rows), lambda b, j, cc: (j, 0, 0))],
        out_specs=[ospec, ospec],
        out_shape=[out, out],
        scratch_shapes=[pltpu.VMEM((nb, kout, tc), BF), pltpu.VMEM((nb, kout, tc), BF)],
        compiler_params=_params("parallel", "arbitrary", "arbitrary"),
        name=name,
    )(x, table)


def _filter_inner_kernel(ur, ui, d, ss, kr_ref, ki_ref, *, kb, n2):
    scale = lax.rsqrt(ss[...] + EPS)
    urt = pltpu.einshape("nkc->knc", ur[...])
    uit = pltpu.einshape("nkc->knc", ui[...])
    for k in range(kb):
        rows = pl.ds(k * n2, n2)
        cat = jnp.concatenate([urt[k], uit[k]], axis=0)
        x = jnp.dot(d[...], cat, preferred_element_type=F32) * scale
        kr_ref[rows, :] = x[:n2].astype(kr_ref.dtype)
        ki_ref[rows, :] = x[n2:].astype(ki_ref.dtype)


def filter_inner_stage(ur, ui, d, ss, kb, tc):
    n2, n1, c = ur.shape
    tc = min(tc, c)
    ispec = pl.BlockSpec((n2, kb, tc), lambda i, cc: (0, i, cc))
    ospec = pl.BlockSpec((kb * n2, tc), lambda i, cc: (i, cc))
    out = jax.ShapeDtypeStruct((n1 * n2, c), BF)
    return pl.pallas_call(
        functools.partial(_filter_inner_kernel, kb=kb, n2=n2),
        grid=(n1 // kb, c // tc),
        in_specs=[ispec, ispec, pl.BlockSpec((2 * n2, 2 * n2), lambda i, cc: (0, 0)),
                  pl.BlockSpec((1, tc), lambda i, cc: (0, cc))],
        out_specs=[ospec, ospec],
        out_shape=[out, out],
        compiler_params=_params("parallel", "arbitrary"),
        name="filter_inner_dft",
    )(ur, ui, d, ss)


def _hyena_mid_kernel(ur, ui, kr, ki, d, minv, qr_ref, qi_ref, *, kb, n2):
    for k in range(kb):
        rows = pl.ds(k * n2, n2)
        cat = jnp.concatenate([ur[rows, :], ui[rows, :]], axis=0)
        x = jnp.dot(d[...], cat, preferred_element_type=F32)
        xr, xi = x[:n2], x[n2:]
        fr, fi = kr[rows, :].astype(F32), ki[rows, :].astype(F32)
        pr = xr * fr - xi * fi
        pi = xr * fi + xi * fr
        catp = jnp.concatenate([pr, pi], axis=0).astype(BF)
        q = jnp.dot(minv[k], catp, preferred_element_type=F32)
        qr_ref[rows, :] = q[:n2].astype(qr_ref.dtype)
        qi_ref[rows, :] = q[n2:].astype(qi_ref.dtype)


def hyena_mid_stage(ur, ui, kfr, kfi, d, minv, order, n2, kb):
    n, c = ur.shape
    n1 = n // n2
    kb = min(kb, n1)
    spec = pl.BlockSpec((kb * n2, c), lambda i: (i, 0))
    kspec = pl.BlockSpec((kb * n2, c), lambda i: (i, order))
    out = jax.ShapeDtypeStruct((n, c), BF)
    return pl.pallas_call(
        functools.partial(_hyena_mid_kernel, kb=kb, n2=n2),
        grid=(n1 // kb,),
        in_specs=[spec, spec, kspec, kspec,
                  pl.BlockSpec((2 * n2, 2 * n2), lambda i: (0, 0)),
                  pl.BlockSpec((kb, 2 * n2, 2 * n2), lambda i: (i, 0, 0))],
        out_specs=[spec, spec],
        out_shape=[out, out],
        compiler_params=_params("parallel"),
        name="hyena_mid",
    )(ur, ui, kfr, kfi, d, minv)


def _hyena_out_kernel(qr, qi, g, xo, z, skip, o_ref, y_s, *, nb):
    qrt = pltpu.einshape("nrc->rnc", qr[...])
    qit = pltpu.einshape("nrc->rnc", qi[...])
    for r in range(nb):
        cat = jnp.concatenate([qrt[r], qit[r]], axis=0)
        y_s[r] = jnp.dot(g[...], cat, preferred_element_type=F32)
    y = pltpu.einshape("rnc->nrc", y_s[...])
    zf = z[...].astype(F32)
    o_ref[...] = (xo[...].astype(F32) * (y + skip[...] * zf)).astype(o_ref.dtype)


def hyena_out_stage(qr, qi, g, xo, z, skip, nb, tc):
    rows, n2, c = qr.shape
    tc = min(tc, c)
    spec = pl.BlockSpec((rows, nb, tc), lambda j, cc: (0, j, cc))
    return pl.pallas_call(
        functools.partial(_hyena_out_kernel, nb=nb),
        grid=(n2 // nb, c // tc),
        in_specs=[spec, spec, pl.BlockSpec((rows, 2 * rows), lambda j, cc: (0, 0)), spec, spec,
                  pl.BlockSpec((1, tc), lambda j, cc: (0, cc))],
        out_specs=spec,
        out_shape=jax.ShapeDtypeStruct((rows, n2, c), BF),
        scratch_shapes=[pltpu.VMEM((nb, rows, tc), F32)],
        compiler_params=_params("parallel", "arbitrary"),
        name="hyena_out",
    )(qr, qi, g, xo, z, skip)


def _dot_bf16x3(a, b):
    ah = a.astype(BF)
    bh = b.astype(BF)
    al = (a - ah.astype(F32)).astype(BF)
    bl = (b - bh.astype(F32)).astype(BF)
    return jnp.dot(jnp.concatenate([ah, al, ah], axis=1), jnp.concatenate([bh, bh, bl], axis=0),
                   preferred_element_type=F32)


def _filter_kernel(ft, w1, b1, f1, w2, b2, f2, w3, decf, decb, tab, ur_ref, ui_ref, ss_ref, *, n1):
    hi = lax.Precision.HIGHEST
    x = ft[...]
    ffn2 = w2.shape[1]
    h = jnp.sin(f1[...] * (jnp.dot(x, w1[...], precision=hi, preferred_element_type=F32) + b1[...]))
    h = jnp.sin(f2[...] * (jnp.dot(h, w2[...], precision=hi, preferred_element_type=F32) + b2[...]))
    t = x[:, 0:1]
    is_f = x[:, FWD_COL:FWD_COL + 1]
    is_b = x[:, BWD_COL:BWD_COL + 1]
    lane = lax.broadcasted_iota(jnp.int32, h.shape, 1)
    h = h * jnp.where(lane < ffn2 // 2, is_f, is_b)
    h = _dot_bf16x3(h, w3[...])
    dec = jnp.where(is_f > 0.0, jnp.abs(decf[...]), jnp.abs(decb[...]))
    k = h * jnp.exp(-t * dec)
    kb = k.astype(BF)
    for r in range(tab.shape[0]):
        rows = pl.ds(r * n1, n1)
        u = jnp.dot(tab[r], kb[r * n1:(r + 1) * n1], preferred_element_type=F32)
        ur_ref[rows, :] = u[:n1].astype(ur_ref.dtype)
        ui_ref[rows, :] = u[n1:].astype(ui_ref.dtype)
    part = jnp.sum(k * k, axis=0, keepdims=True)

    @pl.when(pl.program_id(0) == 0)
    def _():
        ss_ref[...] = jnp.zeros_like(ss_ref)

    ss_ref[...] += jnp.broadcast_to(part, ss_ref.shape)


def hyena_filter_taps(feats, w1, b1, f1, w2, b2, f2, w3, decf, decb, table, nb):
    n, fp = feats.shape
    ffn = w1.shape[1]
    ffn2 = w2.shape[1]
    oc = w3.shape[-1]
    _, two_n1, n1 = table.shape
    tr = nb * n1
    small = lambda shape: pl.BlockSpec(shape, lambda i: (0,) * len(shape))
    ospec = pl.BlockSpec((tr, oc), lambda i: (i, 0))
    out = jax.ShapeDtypeStruct((n, oc), BF)
    return pl.pallas_call(
        functools.partial(_filter_kernel, n1=n1),
        grid=(n // tr,),
        in_specs=[pl.BlockSpec((tr, fp), lambda i: (i, 0)),
                  small((fp, ffn)), small((1, ffn)), small((1, ffn)),
                  small((ffn, ffn2)), small((1, ffn2)), small((1, ffn2)),
                  small((ffn2, oc)), small((1, oc)), small((1, oc)),
                  pl.BlockSpec((nb, two_n1, n1), lambda i: (i, 0, 0))],
        out_specs=[ospec, ospec, pl.BlockSpec((8, oc), lambda i: (0, 0))],
        out_shape=[out, out, jax.ShapeDtypeStruct((8, oc), F32)],
        compiler_params=_params("arbitrary"),
        name="hyena_filter_taps",
    )(feats, w1, b1, f1, w2, b2, f2, w3, decf, decb, table)


def _fnet_kernel(yr, yi, d2, csgw, b, o_ref, xcat, o_s, *, kb, n2, c):
    for k in range(kb):
        rows = pl.ds(k * n2, n2)
        cat = jnp.concatenate([yr[rows, :], yi[rows, :]], axis=0)
        x = jnp.dot(d2[...], cat, preferred_element_type=F32)
        xcat[rows, pl.ds(0, c)] = x[:n2].astype(BF)
        xcat[rows, pl.ds(c, c)] = x[n2:].astype(BF)
    out = jnp.dot(xcat[...], csgw[...], preferred_element_type=F32) + b[...]
    for k in range(kb):
        o_s[k] = out[k * n2:(k + 1) * n2].astype(o_s.dtype)
    o_ref[...] = pltpu.einshape("kmc->mkc", o_s[...])


def fnet_inner_stage(yr, yi, d2, csgw, b, n2, kb):
    bsz, l, c = yr.shape
    n1 = l // n2
    kb = min(kb, n1)
    spec = pl.BlockSpec((None, kb * n2, c), lambda bb, i: (bb, i, 0))
    const = lambda shape: pl.BlockSpec(shape, lambda bb, i: (0,) * len(shape))
    return pl.pallas_call(
        functools.partial(_fnet_kernel, kb=kb, n2=n2, c=c),
        grid=(bsz, n1 // kb),
        in_specs=[spec, spec, const((2 * n2, 2 * n2)), const((2 * c, c)), const((1, c))],
        out_specs=pl.BlockSpec((None, n2, kb, c), lambda bb, i: (bb, 0, i, 0)),
        out_shape=jax.ShapeDtypeStruct((bsz, n2, n1, c), BF),
        scratch_shapes=[pltpu.VMEM((kb * n2, 2 * c), BF), pltpu.VMEM((kb, n2, c), BF)],
        compiler_params=_params("parallel", "arbitrary"),
        name="fnet_inner",
    )(yr, yi, d2, csgw, b)


def _attn_kernel(q_ref, k_ref, v_ref, o_ref, *, heads):
    c = q_ref.shape[1]
    hd = c // heads
    scale = hd ** -0.5
    for h in range(heads):
        cols = pl.ds(h * hd, hd)
        s = lax.dot_general(q_ref[:, cols], k_ref[:, cols], (((1,), (1,)), ((), ())),
                            preferred_element_type=F32) * scale
        s = s - jnp.max(s, axis=-1, keepdims=True)
        e = jnp.exp(s)
        p = e / jnp.sum(e, axis=-1, keepdims=True)
        o = jnp.dot(p.astype(BF), v_ref[:, cols], preferred_element_type=F32)
        o_ref[:, cols] = o.astype(o_ref.dtype)


def memory_attention(q, kh, vh, seq, tr, col):
    m = q.shape[0]
    _, nm, c = kh.shape
    per = seq // tr
    kv = pl.BlockSpec((None, nm, c), lambda i: (i // per, 0, 0))
    return pl.pallas_call(
        functools.partial(_attn_kernel, heads=MEM_HEADS),
        grid=(m // tr,),
        in_specs=[pl.BlockSpec((tr, c), lambda i: (i, col)), kv, kv],
        out_specs=pl.BlockSpec((tr, c), lambda i: (i, 0)),
        out_shape=jax.ShapeDtypeStruct((m, c), BF),
        compiler_params=_params("parallel"),
        name="memory_attention",
    )(q, kh, vh)


def _merge_kernel(ya, yb, yc, ym, ga, gb, gc, gm, gg, w, x, pg, o_ref):
    g = ya.shape[1]
    acc = None
    for idx, (y_ref, gate) in enumerate(((ya, ga), (yb, gb), (yc, gc), (ym, gm))):
        cols = pl.ds(idx * g, g)
        y = y_ref[...].astype(F32)
        ms = jnp.mean(y * y, axis=-1, keepdims=True)
        yn = y * lax.rsqrt(ms + EPS) * gg[:, cols]
        gt = gate[...]
        yn = yn.astype(BF) * (gt * jax.nn.sigmoid(gt))
        part = jnp.dot(yn, w[cols, :], preferred_element_type=F32)
        acc = part if acc is None else acc + part
    ms = jnp.mean(acc * acc, axis=-1, keepdims=True)
    o_ref[...] = x[...] + acc * lax.rsqrt(ms + EPS) * pg[...]


def merge_project(ya, yb, yc, ym, gate, gate_col, gg, w_out, x, pg, tm):
    m, g = ya.shape
    d = x.shape[1]
    yspec = pl.BlockSpec((tm, g), lambda i: (i, 0))
    gspecs = [pl.BlockSpec((tm, g), lambda i, col=gate_col + k: (i, col)) for k in range(4)]
    const = lambda shape: pl.BlockSpec(shape, lambda i: (0, 0))
    return pl.pallas_call(
        _merge_kernel,
        grid=(m // tm,),
        in_specs=[yspec, yspec, yspec, yspec, *gspecs,
                  const((1, 4 * g)),
                  pl.BlockSpec((4 * g, d), lambda i: (0, 0), pipeline_mode=pl.Buffered(1)),
                  pl.BlockSpec((tm, d), lambda i: (i, 0)),
                  const((1, d))],
        out_specs=pl.BlockSpec((tm, d), lambda i: (i, 0)),
        out_shape=jax.ShapeDtypeStruct((m, d), F32),
        compiler_params=_params("parallel"),
        name="merge_project",
    )(ya, yb, yc, ym, gate, gate, gate, gate, gg, w_out, x, pg)


def _iota(n):
    return np.arange(n, dtype=np.int64)


@functools.lru_cache(maxsize=None)
def _outer_tables(n1, n2, rows_complex):
    n = n1 * n2
    r = _iota(n2)[:, None, None]
    k1 = _iota(n1)[None, :, None]
    nj = n1 // 2 if rows_complex else n1
    j = _iota(nj)[None, None, :]
    er, ei = _cis(k1 * (n2 * j + r), n, -1.0)
    if rows_complex:
        return _bf16_const(_stack_complex(er, ei))
    return _bf16_const(np.concatenate([er, ei], axis=-2))


@functools.lru_cache(maxsize=None)
def _inner_table(n2):
    k = _iota(n2)[:, None]
    j = _iota(n2)[None, :]
    cr, ci = _cis(k * j, n2, -1.0)
    return _bf16_const(_stack_complex(cr, ci))


@functools.lru_cache(maxsize=None)
def _inverse_inner_tables(n1, n2):
    n = n1 * n2
    k1 = _iota(n1)[:, None, None]
    na = _iota(n2)[None, :, None]
    k2 = _iota(n2)[None, None, :]
    ar, ai = _cis(na * (k1 + n1 * k2), n, 1.0)
    return _bf16_const(_stack_complex(ar, ai))


@functools.lru_cache(maxsize=None)
def _inverse_outer_table(n1, n2):
    nb = _iota(n1 // 2)[:, None]
    k1 = _iota(n1)[None, :]
    gr, gi = _cis(nb * k1, n1, 1.0)
    inv = 1.0 / (n1 * n2)
    return _bf16_const(_stack_complex(gr * inv, gi * inv))


@functools.lru_cache(maxsize=None)
def _channel_table(g, scale):
    a = _iota(g)[:, None]
    b = _iota(g)[None, :]
    cg, sg = _cis(a * b, g, 1.0)
    return _bf16_const(np.concatenate([cg, sg], axis=0) * scale)


def _filter_features(positions, seq, n1, n2):
    idx = (n2 * _iota(n1)[None, :] + _iota(n2)[:, None]).reshape(-1)
    pos = positions.astype(F32)
    pos = jnp.concatenate([pos, pos[:1], jnp.flip(pos[1:])])
    pos = pos.reshape(n1, n2).T.reshape(-1)
    t = pos / seq
    bands = jnp.linspace(1e-4, HYENA_BANDS - 1, HYENA_BANDS, dtype=F32)
    w = 2.0 * jnp.pi * pos / seq
    is_f = (idx < seq).astype(F32)
    is_b = (idx > seq).astype(F32)
    feats = jnp.concatenate([t[:, None], jnp.cos(w[:, None] * bands), jnp.sin(w[:, None] * bands),
                             is_f[:, None], is_b[:, None]], axis=-1)
    return jnp.pad(feats, ((0, 0), (0, FEAT_PAD - feats.shape[1])))


def _layer(x2, mem_n, feats, tabs, p, dims):
    bsz, seq, d, g = dims
    m = bsz * seq
    n2 = FFT_INNER
    n1 = 2 * seq // n2
    n1f, n2f = FFT_INNER, seq // FFT_INNER

    proj = input_projection(x2, p["pre_g"], p["w_in"].astype(BF), 1024, g)
    n_in = proj.shape[1]

    y_a = conv_module(proj, p["conv_dw_w"], p["conv_dw_b"], p["conv_ln_g"], p["conv_ln_b"],
                      p["conv_pw_w"], p["conv_pw_b"], seq, 512)

    yr, yi = dft_outer_stage(proj.reshape(bsz, n1f, n2f, n_in), tabs["fnet_outer"], DFT_RESIDUES, 512,
                             "fnet_outer", c=g, col0=2 * g)
    csgw = matmul(tabs["fnet_chan"], p["fnet_w"].astype(BF), 1024, 1024, "fnet_fold")
    y_b = fnet_inner_stage(yr.reshape(bsz, seq, g), yi.reshape(bsz, seq, g), tabs["fnet_inner"],
                           csgw, p["fnet_b"].reshape(1, g), n2f, 16)
    y_b = y_b.reshape(m, g)

    v, x1, x2g = hyena_short_conv(proj, p["hy_short_w"], p["hy_short_b"], seq, 512, 1)
    ffn = p["hy_fw1"].shape[1]
    og = HYENA_ORDER * g
    w1 = jnp.pad(p["hy_fw1"], ((0, FEAT_PAD - p["hy_fw1"].shape[0]), (0, 0)))
    dup = lambda a: jnp.concatenate([a, a], axis=-1)
    w3 = p["hy_fw3"].reshape(ffn, HYENA_ORDER, 2, g)
    w3 = jnp.concatenate([w3[:, :, 0, :].reshape(ffn, og), w3[:, :, 1, :].reshape(ffn, og)], axis=0)
    kur, kui, ss = hyena_filter_taps(
        feats, w1, p["hy_fb1"].reshape(1, ffn), p["hy_freq1"].reshape(1, ffn),
        dup(p["hy_fw2"]), dup(p["hy_fb2"].reshape(1, ffn)), dup(p["hy_freq2"].reshape(1, ffn)),
        w3, p["hy_decay"][:, 0, :].reshape(1, og), p["hy_decay"][:, 1, :].reshape(1, og),
        tabs["filt_outer"], max(1, 512 // n1))
    kfr, kfi = filter_inner_stage(kur.reshape(n2, n1, og), kui.reshape(n2, n1, og), tabs["inner"], ss[0:1],
                                  DFT_RESIDUES, 1024)

    z = v
    for o, gate_o in enumerate((x1, x2g)):
        ur, ui = dft_outer_stage(z.reshape(1, n1, n2, g), tabs["hy_outer"], DFT_RESIDUES, 512, "hyena_outer_dft")
        qr, qi = hyena_mid_stage(ur.reshape(n1 * n2, g), ui.reshape(n1 * n2, g), kfr, kfi,
                                 tabs["inner"], tabs["inv_inner"], o, n2, 8)
        z = hyena_out_stage(qr.reshape(n1, n2, g), qi.reshape(n1, n2, g), tabs["inv_outer"],
                            gate_o.reshape(n1, n2, g), z.reshape(n1, n2, g), p["hy_skip"][o].reshape(1, g),
                            DFT_RESIDUES, 512)
    y_c = z.reshape(m, g)

    nm = mem_n.shape[0] // bsz
    kh = matmul(mem_n, p["mem_wk"].astype(BF), 512, 1024, "mem_k").reshape(bsz, nm, g)
    vh = matmul(mem_n, p["mem_wv"].astype(BF), 512, 1024, "mem_v").reshape(bsz, nm, g)
    y_m = memory_attention(proj, kh, vh, seq, 1024, 6)

    return merge_project(y_a, y_b, y_c, y_m, proj, 7, p["group_g"].reshape(1, 4 * g), p["w_out"].astype(BF),
                         x2, p["post_g"].reshape(1, d), 256)


def kernel(x, mem, positions, mem_norm_g, pre_norm_g, post_norm_g, w_in, conv_dw_w, conv_dw_b, conv_ln_g, conv_ln_b, conv_pw_w, conv_pw_b, fnet_w, fnet_b, hy_short_w, hy_short_b, hy_fw1, hy_fb1, hy_freq1, hy_fw2, hy_fb2, hy_freq2, hy_fw3, hy_decay, hy_skip, mem_wk, mem_wv, group_norm_g, w_out):
    bsz, seq, d = x.shape
    g = conv_pw_w.shape[-1]
    depth = w_in.shape[0]
    assert bsz == 2, "the long convolution packs exactly two batch elements as one complex sequence"
    assert seq % (FFT_INNER * DFT_RESIDUES) == 0
    n2 = FFT_INNER
    n1 = 2 * seq // n2
    n1f, n2f = FFT_INNER, seq // FFT_INNER

    tabs = {
        "fnet_outer": _outer_tables(n1f, n2f, False),
        "fnet_inner": _inner_table(n2f),
        "fnet_chan": _channel_table(g, 1.0 / math.sqrt(seq * g)),
        "hy_outer": _outer_tables(n1, n2, True),
        "filt_outer": _outer_tables(n1, n2, False),
        "inner": _inner_table(n2),
        "inv_inner": _inverse_inner_tables(n1, n2),
        "inv_outer": _inverse_outer_table(n1, n2),
    }
    tabs = {name: jnp.asarray(table) for name, table in tabs.items()}
    feats = _filter_features(positions, seq, n1, n2)
    mem_n = rmsnorm_bf16(mem.reshape(-1, d), mem_norm_g, min(512, mem.shape[0] * mem.shape[1]))

    x2 = x.reshape(bsz * seq, d)
    for l in range(depth):
        p = {
            "pre_g": pre_norm_g[l], "post_g": post_norm_g[l], "w_in": w_in[l],
            "conv_dw_w": conv_dw_w[l], "conv_dw_b": conv_dw_b[l], "conv_ln_g": conv_ln_g[l],
            "conv_ln_b": conv_ln_b[l], "conv_pw_w": conv_pw_w[l], "conv_pw_b": conv_pw_b[l],
            "fnet_w": fnet_w[l], "fnet_b": fnet_b[l], "hy_short_w": hy_short_w[l], "hy_short_b": hy_short_b[l],
            "hy_fw1": hy_fw1[l], "hy_fb1": hy_fb1[l], "hy_freq1": hy_freq1[l], "hy_fw2": hy_fw2[l],
            "hy_fb2": hy_fb2[l], "hy_freq2": hy_freq2[l], "hy_fw3": hy_fw3[l], "hy_decay": hy_decay[l],
            "hy_skip": hy_skip[l], "mem_wk": mem_wk[l], "mem_wv": mem_wv[l], "group_g": group_norm_g[l],
            "w_out": w_out[l],
        }
        x2 = _layer(x2, mem_n, feats, tabs, p, (bsz, seq, d, g))
    return x2.reshape(bsz, seq, d)
```

```python
import functools
import math

import jax
import jax.numpy as jnp
import numpy as np
from jax import lax
from jax.experimental import pallas as pl
from jax.experimental.pallas import tpu as pltpu

BF = jnp.bfloat16
F32 = jnp.float32
EPS = 1e-6

CONV_WIDTH = 31
CONV_HALF = CONV_WIDTH // 2
HALO = 16
SUBLANES, LANES = 8, 128
HYENA_BANDS = 16
HYENA_ORDER = 2
MEM_HEADS = 4
FFT_INNER = 128
FEAT_PAD = 128
FWD_COL = 2 * HYENA_BANDS + 1
BWD_COL = 2 * HYENA_BANDS + 2
DFT_RESIDUES = 16
VMEM_LIMIT = 56 * 1024 * 1024


def _params(*sem):
    return pltpu.CompilerParams(dimension_semantics=sem, vmem_limit_bytes=VMEM_LIMIT)


def _rmsnorm_kernel(x_ref, g_ref, o_ref):
    x = x_ref[...]
    ms = jnp.mean(x * x, axis=-1, keepdims=True)
    o_ref[...] = (x * lax.rsqrt(ms + EPS) * g_ref[...]).astype(o_ref.dtype)


def rmsnorm_bf16(x, g, tm):
    m, d = x.shape
    return pl.pallas_call(
        _rmsnorm_kernel,
        grid=(m // tm,),
        in_specs=[pl.BlockSpec((tm, d), lambda i: (i, 0)),
                  pl.BlockSpec((1, d), lambda i: (0, 0))],
        out_specs=pl.BlockSpec((tm, d), lambda i: (i, 0)),
        out_shape=jax.ShapeDtypeStruct((m, d), BF),
        compiler_params=_params("parallel"),
        name="rmsnorm",
    )(x, g.reshape(1, d))


def _mm_kernel(a_ref, w_ref, o_ref):
    o_ref[...] = jnp.dot(a_ref[...].astype(BF), w_ref[...].astype(BF),
                         preferred_element_type=F32).astype(o_ref.dtype)


def matmul(a, w, tm, tn, name, col0=0, n=None):
    m, k = a.shape
    n = w.shape[1] if n is None else n
    tm, tn = min(tm, m), min(tn, n)
    joff = col0 // tn
    return pl.pallas_call(
        _mm_kernel,
        grid=(m // tm, n // tn),
        in_specs=[pl.BlockSpec((tm, k), lambda i, j: (i, 0)),
                  pl.BlockSpec((k, tn), lambda i, j: (0, j + joff))],
        out_specs=pl.BlockSpec((tm, tn), lambda i, j: (i, j)),
        out_shape=jax.ShapeDtypeStruct((m, n), BF),
        compiler_params=_params("parallel", "arbitrary"),
        name=name,
    )(a, w)


def _proj_kernel(x_ref, g_ref, w_ref, o_ref, h_s):
    @pl.when(pl.program_id(1) == 0)
    def _():
        x = x_ref[...]
        ms = jnp.mean(x * x, axis=-1, keepdims=True)
        h_s[...] = (x * lax.rsqrt(ms + EPS) * g_ref[...]).astype(h_s.dtype)

    o_ref[...] = jnp.dot(h_s[...], w_ref[...], preferred_element_type=F32).astype(o_ref.dtype)


def input_projection(x, g, w, tm, tn):
    m, d = x.shape
    n = w.shape[1]
    return pl.pallas_call(
        _proj_kernel,
        grid=(m // tm, n // tn),
        in_specs=[pl.BlockSpec((tm, d), lambda i, j: (i, 0)),
                  pl.BlockSpec((1, d), lambda i, j: (0, 0)),
                  pl.BlockSpec((d, tn), lambda i, j: (0, j))],
        out_specs=pl.BlockSpec((tm, tn), lambda i, j: (i, j)),
        out_shape=jax.ShapeDtypeStruct((m, n), BF),
        scratch_shapes=[pltpu.VMEM((tm, d), BF)],
        compiler_params=_params("parallel", "arbitrary"),
        name="input_projection",
    )(x, g.reshape(1, d), w)


def _halo_specs(tr, width, col):
    per = tr // HALO

    def cur(i):
        return (i, col)

    def prev(i):
        return (jnp.maximum(i * per - 1, 0), col)

    def make_next(nblocks):
        def nxt(i):
            return (jnp.minimum((i + 1) * per, nblocks - 1), col)
        return nxt

    return cur, prev, make_next


def _seq_edges(i, tr, seq):
    tiles = seq // tr
    pos = i % tiles
    return pos == 0, pos == tiles - 1


def _conva_kernel(vc, gc, vp, gp, vn, gn, dww, dwb, lng, lnb, pww, pwb, o_ref, upad, cbuf, wbc, pw_s, *, seq):
    tr, c = vc.shape
    first, last = _seq_edges(pl.program_id(0), tr, seq)

    @pl.when(pl.program_id(0) == 0)
    def _():
        pw_s[...] = pww[...].astype(pw_s.dtype)

    def glu(v, g):
        return v[...].astype(F32) * jax.nn.sigmoid(g[...].astype(F32))

    upad[pl.ds(HALO, tr), :] = glu(vc, gc)
    upad[pl.ds(0, HALO), :] = jnp.where(first, 0.0, glu(vp, gp))
    upad[pl.ds(HALO + tr, HALO), :] = jnp.where(last, 0.0, glu(vn, gn))

    for j in range(CONV_WIDTH):
        wbc[pl.ds(SUBLANES * j, SUBLANES), :] = jnp.broadcast_to(dww[pl.ds(j, 1), :], (SUBLANES, c))
    row = lax.broadcasted_iota(jnp.int32, (SUBLANES, LANES), 0)
    for ct in range(c // LANES):
        lanes = pl.ds(ct * LANES, LANES)

        def residue_sums(b, lanes=lanes):
            u = [upad[pl.ds(pl.multiple_of((b + a) * SUBLANES, SUBLANES), SUBLANES), lanes] for a in range(4)]
            sums = []
            for s in range(SUBLANES):
                acc = None
                for a in range(4):
                    j = SUBLANES * a + s - 1
                    if 0 <= j < CONV_WIDTH:
                        term = u[a] * wbc[pl.ds(SUBLANES * j, SUBLANES), lanes]
                        acc = term if acc is None else acc + term
                sums.append(acc)
            return tuple(sums)

        bias = jnp.broadcast_to(dwb[:, lanes], (SUBLANES, LANES))

        def body(b, prev, lanes=lanes, bias=bias, residue_sums=residue_sums):
            cur = residue_sums(b)
            terms = [bias + prev[0]]
            for s in range(1, SUBLANES):
                terms.append(pltpu.roll(jnp.where(row >= s, prev[s], cur[s]), SUBLANES - s, axis=0))
            while len(terms) > 1:
                terms = [terms[i] + terms[i + 1] for i in range(0, len(terms), 2)]
            out = terms[0]
            cbuf[pl.ds(pl.multiple_of((b - 1) * SUBLANES, SUBLANES), SUBLANES), lanes] = out
            return cur

        lax.fori_loop(1, tr // SUBLANES + 1, body, residue_sums(0), unroll=4)

    u = cbuf[...]
    mu = jnp.mean(u, axis=-1, keepdims=True)
    d = u - mu
    var = jnp.mean(d * d, axis=-1, keepdims=True)
    y = d * lax.rsqrt(var + EPS) * lng[...] + lnb[...]
    y = y * jax.nn.sigmoid(y)
    out = jnp.dot(y.astype(BF), pw_s[...], preferred_element_type=F32) + pwb[...]
    o_ref[...] = out.astype(o_ref.dtype)


def conv_module(a, dww, dwb, lng, lnb, pww, pwb, seq, tr):
    m = a.shape[0]
    c = pww.shape[0]
    cur, prev, make_next = _halo_specs(tr, c, 0)
    cur_g, prev_g, make_next_g = _halo_specs(tr, c, 1)
    nh = m // HALO
    row = lambda i: (0, 0)
    return pl.pallas_call(
        functools.partial(_conva_kernel, seq=seq),
        grid=(m // tr,),
        in_specs=[pl.BlockSpec((tr, c), cur), pl.BlockSpec((tr, c), cur_g),
                  pl.BlockSpec((HALO, c), prev), pl.BlockSpec((HALO, c), prev_g),
                  pl.BlockSpec((HALO, c), make_next(nh)), pl.BlockSpec((HALO, c), make_next_g(nh)),
                  pl.BlockSpec((CONV_WIDTH, c), row), pl.BlockSpec((1, c), row),
                  pl.BlockSpec((1, c), row), pl.BlockSpec((1, c), row),
                  pl.BlockSpec((c, c), row), pl.BlockSpec((1, c), row)],
        out_specs=pl.BlockSpec((tr, c), lambda i: (i, 0)),
        out_shape=jax.ShapeDtypeStruct((m, c), BF),
        scratch_shapes=[pltpu.VMEM((tr + 2 * HALO, c), F32), pltpu.VMEM((tr, c), F32),
                        pltpu.VMEM((CONV_WIDTH * SUBLANES, c), F32), pltpu.VMEM((c, c), BF)],
        compiler_params=_params("arbitrary"),
        name="conv_module",
    )(a, a, a, a, a, a, dww, dwb.reshape(1, c), lng.reshape(1, c), lnb.reshape(1, c),
      pww, pwb.reshape(1, c))


def _short_kernel(hc, hp, hn, w, b, v_ref, x1_ref, x2_ref, *, seq):
    tr, c3 = hc.shape
    c = c3 // 3
    first, last = _seq_edges(pl.program_id(0), tr, seq)
    edge = lax.broadcasted_iota(jnp.int32, (HALO, c), 0)
    for g, o_ref in enumerate((v_ref, x1_ref, x2_ref)):
        cols = pl.ds(g * c, c)
        w0, w1, w2, bias = w[pl.ds(0, 1), cols], w[pl.ds(1, 1), cols], w[pl.ds(2, 1), cols], b[:, cols]
        cur = hc[:, cols].astype(F32)
        before = pltpu.roll(cur, 1, axis=0)
        after = pltpu.roll(cur, tr - 1, axis=0)
        o_ref[...] = (before * w0 + cur * w1 + after * w2 + bias).astype(o_ref.dtype)

        prev_row = jnp.where(first, 0.0, hp[pl.ds(HALO - 1, 1), cols].astype(F32))
        next_row = jnp.where(last, 0.0, hn[pl.ds(0, 1), cols].astype(F32))
        head = jnp.where(edge == 0, prev_row, before[:HALO])
        tail = jnp.where(edge == HALO - 1, next_row, after[tr - HALO:])
        o_ref[pl.ds(0, HALO), :] = (head * w0 + cur[:HALO] * w1 + after[:HALO] * w2 + bias).astype(o_ref.dtype)
        o_ref[pl.ds(tr - HALO, HALO), :] = (before[tr - HALO:] * w0 + cur[tr - HALO:] * w1 + tail * w2
                                            + bias).astype(o_ref.dtype)


def hyena_short_conv(hy, w, b, seq, tr, col):
    m = hy.shape[0]
    c3 = w.shape[1]
    c = c3 // 3
    cur, prev, make_next = _halo_specs(tr, c3, col)
    out = jax.ShapeDtypeStruct((m, c), BF)
    ospec = pl.BlockSpec((tr, c), lambda i: (i, 0))
    return pl.pallas_call(
        functools.partial(_short_kernel, seq=seq),
        grid=(m // tr,),
        in_specs=[pl.BlockSpec((tr, c3), cur), pl.BlockSpec((HALO, c3), prev),
                  pl.BlockSpec((HALO, c3), make_next(m // HALO)),
                  pl.BlockSpec((3, c3), lambda i: (0, 0)), pl.BlockSpec((1, c3), lambda i: (0, 0))],
        out_specs=[ospec, ospec, ospec],
        out_shape=[out, out, out],
        compiler_params=_params("parallel"),
        name="hyena_short_conv",
    )(hy, hy, hy, w, b.reshape(1, c3))


def _cis(p, n, sign):
    ang = (2.0 * np.pi / n) * (p % n)
    return np.cos(ang), sign * np.sin(ang)


def _stack_complex(mr, mi):
    return np.concatenate([np.concatenate([mr, -mi], axis=-1),
                           np.concatenate([mi, mr], axis=-1)], axis=-2)


def _bf16_const(a):
    return a.astype(jnp.bfloat16)


def _outer_stage_kernel(x_ref, t_ref, or_ref, oi_ref, yr_s, yi_s, *, nb):
    xt = pltpu.einshape("nrc->rnc", x_ref[...])
    for r in range(nb):
        y = jnp.dot(t_ref[r], xt[r], preferred_element_type=F32)
        h = y.shape[0] // 2
        yr_s[r] = y[:h].astype(BF)
        yi_s[r] = y[h:].astype(BF)
    or_ref[...] = pltpu.einshape("rnc->nrc", yr_s[...])
    oi_ref[...] = pltpu.einshape("rnc->nrc", yi_s[...])


def dft_outer_stage(x, table, nb, tc, name, c=None, col0=0):
    bsz, rows, n_inner, _ = x.shape
    c = x.shape[3] if c is None else c
    _, two_k, _ = table.shape
    kout = two_k // 2
    tc = min(tc, c)
    coff = col0 // tc
    out = jax.ShapeDtypeStruct((bsz, kout, n_inner, c), BF)
    ospec = pl.BlockSpec((None, kout, nb, tc), lambda b, j, cc: (b, 0, j, cc))
    return pl.pallas_call(
        functools.partial(_outer_stage_kernel, nb=nb),
        grid=(bsz, n_inner // nb, c // tc),
        in_specs=[pl.BlockSpec((None, rows, nb, tc), lambda b, j, cc: (b, 0, j, cc + coff)),
                  pl.BlockSpec((nb, two_k, rows), lambda b, j, cc: (j, 0, 0))],
        out_specs=[ospec, ospec],
        out_shape=[out, out],
        scratch_shapes=[pltpu.VMEM((nb, kout, tc), BF), pltpu.VMEM((nb, kout, tc), BF)],
        compiler_params=_params("parallel", "arbitrary", "arbitrary"),
        name=name,
    )(x, table)


def _filter_inner_kernel(ur, ui, d, ss, kr_ref, ki_ref, *, kb, n2):
    scale = lax.rsqrt(ss[...] + EPS)
    urt = pltpu.einshape("nkc->knc", ur[...])
    uit = pltpu.einshape("nkc->knc", ui[...])
    for k in range(kb):
        rows = pl.ds(k * n2, n2)
        cat = jnp.concatenate([urt[k], uit[k]], axis=0)
        x = jnp.dot(d[...], cat, preferred_element_type=F32) * scale
        kr_ref[rows, :] = x[:n2].astype(kr_ref.dtype)
        ki_ref[rows, :] = x[n2:].astype(ki_ref.dtype)


def filter_inner_stage(ur, ui, d, ss, kb, tc):
    n2, n1, c = ur.shape
    tc = min(tc, c)
    ispec = pl.BlockSpec((n2, kb, tc), lambda i, cc: (0, i, cc))
    ospec = pl.BlockSpec((kb * n2, tc), lambda i, cc: (i, cc))
    out = jax.ShapeDtypeStruct((n1 * n2, c), BF)
    return pl.pallas_call(
        functools.partial(_filter_inner_kernel, kb=kb, n2=n2),
        grid=(n1 // kb, c // tc),
        in_specs=[ispec, ispec, pl.BlockSpec((2 * n2, 2 * n2), lambda i, cc: (0, 0)),
                  pl.BlockSpec((1, tc), lambda i, cc: (0, cc))],
        out_specs=[ospec, ospec],
        out_shape=[out, out],
        compiler_params=_params("parallel", "arbitrary"),
        name="filter_inner_dft",
    )(ur, ui, d, ss)


def _hyena_mid_kernel(ur, ui, kr, ki, d, minv, qr_ref, qi_ref, *, kb, n2):
    for k in range(kb):
        rows = pl.ds(k * n2, n2)
        cat = jnp.concatenate([ur[rows, :], ui[rows, :]], axis=0)
        x = jnp.dot(d[...], cat, preferred_element_type=F32)
        xr, xi = x[:n2], x[n2:]
        fr, fi = kr[rows, :].astype(F32), ki[rows, :].astype(F32)
        pr = xr * fr - xi * fi
        pi = xr * fi + xi * fr
        catp = jnp.concatenate([pr, pi], axis=0).astype(BF)
        q = jnp.dot(minv[k], catp, preferred_element_type=F32)
        qr_ref[rows, :] = q[:n2].astype(qr_ref.dtype)
        qi_ref[rows, :] = q[n2:].astype(qi_ref.dtype)


def hyena_mid_stage(ur, ui, kfr, kfi, d, minv, order, n2, kb):
    n, c = ur.shape
    n1 = n // n2
    kb = min(kb, n1)
    spec = pl.BlockSpec((kb * n2, c), lambda i: (i, 0))
    kspec = pl.BlockSpec((kb * n2, c), lambda i: (i, order))
    out = jax.ShapeDtypeStruct((n, c), BF)
    return pl.pallas_call(
        functools.partial(_hyena_mid_kernel, kb=kb, n2=n2),
        grid=(n1 // kb,),
        in_specs=[spec, spec, kspec, kspec,
                  pl.BlockSpec((2 * n2, 2 * n2), lambda i: (0, 0)),
                  pl.BlockSpec((kb, 2 * n2, 2 * n2), lambda i: (i, 0, 0))],
        out_specs=[spec, spec],
        out_shape=[out, out],
        compiler_params=_params("parallel"),
        name="hyena_mid",
    )(ur, ui, kfr, kfi, d, minv)


def _hyena_out_kernel(qr, qi, g, xo, z, skip, o_ref, y_s, *, nb):
    qrt = pltpu.einshape("nrc->rnc", qr[...])
    qit = pltpu.einshape("nrc->rnc", qi[...])
    for r in range(nb):
        cat = jnp.concatenate([qrt[r], qit[r]], axis=0)
        y_s[r] = jnp.dot(g[...], cat, preferred_element_type=F32)
    y = pltpu.einshape("rnc->nrc", y_s[...])
    zf = z[...].astype(F32)
    o_ref[...] = (xo[...].astype(F32) * (y + skip[...] * zf)).astype(o_ref.dtype)


def hyena_out_stage(qr, qi, g, xo, z, skip, nb, tc):
    rows, n2, c = qr.shape
    tc = min(tc, c)
    spec = pl.BlockSpec((rows, nb, tc), lambda j, cc: (0, j, cc))
    return pl.pallas_call(
        functools.partial(_hyena_out_kernel, nb=nb),
        grid=(n2 // nb, c // tc),
        in_specs=[spec, spec, pl.BlockSpec((rows, 2 * rows), lambda j, cc: (0, 0)), spec, spec,
                  pl.BlockSpec((1, tc), lambda j, cc: (0, cc))],
        out_specs=spec,
        out_shape=jax.ShapeDtypeStruct((rows, n2, c), BF),
        scratch_shapes=[pltpu.VMEM((nb, rows, tc), F32)],
        compiler_params=_params("parallel", "arbitrary"),
        name="hyena_out",
    )(qr, qi, g, xo, z, skip)


def _dot_bf16x3(a, b):
    ah = a.astype(BF)
    bh = b.astype(BF)
    al = (a - ah.astype(F32)).astype(BF)
    bl = (b - bh.astype(F32)).astype(BF)
    return jnp.dot(jnp.concatenate([ah, al, ah], axis=1), jnp.concatenate([bh, bh, bl], axis=0),
                   preferred_element_type=F32)


def _filter_kernel(ft, w1, b1, f1, w2, b2, f2, w3, decf, decb, tab, ur_ref, ui_ref, ss_ref, *, n1):
    hi = lax.Precision.HIGHEST
    x = ft[...]
    ffn2 = w2.shape[1]
    h = jnp.sin(f1[...] * (jnp.dot(x, w1[...], precision=hi, preferred_element_type=F32) + b1[...]))
    h = jnp.sin(f2[...] * (jnp.dot(h, w2[...], precision=hi, preferred_element_type=F32) + b2[...]))
    t = x[:, 0:1]
    is_f = x[:, FWD_COL:FWD_COL + 1]
    is_b = x[:, BWD_COL:BWD_COL + 1]
    lane = lax.broadcasted_iota(jnp.int32, h.shape, 1)
    h = h * jnp.where(lane < ffn2 // 2, is_f, is_b)
    h = _dot_bf16x3(h, w3[...])
    dec = jnp.where(is_f > 0.0, jnp.abs(decf[...]), jnp.abs(decb[...]))
    k = h * jnp.exp(-t * dec)
    kb = k.astype(BF)
    for r in range(tab.shape[0]):
        rows = pl.ds(r * n1, n1)
        u = jnp.dot(tab[r], kb[r * n1:(r + 1) * n1], preferred_element_type=F32)
        ur_ref[rows, :] = u[:n1].astype(ur_ref.dtype)
        ui_ref[rows, :] = u[n1:].astype(ui_ref.dtype)
    part = jnp.sum(k * k, axis=0, keepdims=True)

    @pl.when(pl.program_id(0) == 0)
    def _():
        ss_ref[...] = jnp.zeros_like(ss_ref)

    ss_ref[...] += jnp.broadcast_to(part, ss_ref.shape)


def hyena_filter_taps(feats, w1, b1, f1, w2, b2, f2, w3, decf, decb, table, nb):
    n, fp = feats.shape
    ffn = w1.shape[1]
    ffn2 = w2.shape[1]
    oc = w3.shape[-1]
    _, two_n1, n1 = table.shape
    tr = nb * n1
    small = lambda shape: pl.BlockSpec(shape, lambda i: (0,) * len(shape))
    ospec = pl.BlockSpec((tr, oc), lambda i: (i, 0))
    out = jax.ShapeDtypeStruct((n, oc), BF)
    return pl.pallas_call(
        functools.partial(_filter_kernel, n1=n1),
        grid=(n // tr,),
        in_specs=[pl.BlockSpec((tr, fp), lambda i: (i, 0)),
                  small((fp, ffn)), small((1, ffn)), small((1, ffn)),
                  small((ffn, ffn2)), small((1, ffn2)), small((1, ffn2)),
                  small((ffn2, oc)), small((1, oc)), small((1, oc)),
                  pl.BlockSpec((nb, two_n1, n1), lambda i: (i, 0, 0))],
        out_specs=[ospec, ospec, pl.BlockSpec((8, oc), lambda i: (0, 0))],
        out_shape=[out, out, jax.ShapeDtypeStruct((8, oc), F32)],
        compiler_params=_params("arbitrary"),
        name="hyena_filter_taps",
    )(feats, w1, b1, f1, w2, b2, f2, w3, decf, decb, table)


def _fnet_kernel(yr, yi, d2, csgw, b, o_ref, xcat, o_s, *, kb, n2, c):
    for k in range(kb):
        rows = pl.ds(k * n2, n2)
        cat = jnp.concatenate([yr[rows, :], yi[rows, :]], axis=0)
        x = jnp.dot(d2[...], cat, preferred_element_type=F32)
        xcat[rows, pl.ds(0, c)] = x[:n2].astype(BF)
        xcat[rows, pl.ds(c, c)] = x[n2:].astype(BF)
    out = jnp.dot(xcat[...], csgw[...], preferred_element_type=F32) + b[...]
    for k in range(kb):
        o_s[k] = out[k * n2:(k + 1) * n2].astype(o_s.dtype)
    o_ref[...] = pltpu.einshape("kmc->mkc", o_s[...])


def fnet_inner_stage(yr, yi, d2, csgw, b, n2, kb):
    bsz, l, c = yr.shape
    n1 = l // n2
    kb = min(kb, n1)
    spec = pl.BlockSpec((None, kb * n2, c), lambda bb, i: (bb, i, 0))
    const = lambda shape: pl.BlockSpec(shape, lambda bb, i: (0,) * len(shape))
    return pl.pallas_call(
        functools.partial(_fnet_kernel, kb=kb, n2=n2, c=c),
        grid=(bsz, n1 // kb),
        in_specs=[spec, spec, const((2 * n2, 2 * n2)), const((2 * c, c)), const((1, c))],
        out_specs=pl.BlockSpec((None, n2, kb, c), lambda bb, i: (bb, 0, i, 0)),
        out_shape=jax.ShapeDtypeStruct((bsz, n2, n1, c), BF),
        scratch_shapes=[pltpu.VMEM((kb * n2, 2 * c), BF), pltpu.VMEM((kb, n2, c), BF)],
        compiler_params=_params("parallel", "arbitrary"),
        name="fnet_inner",
    )(yr, yi, d2, csgw, b)


def _attn_kernel(q_ref, k_ref, v_ref, o_ref, *, heads):
    c = q_ref.shape[1]
    hd = c // heads
    scale = hd ** -0.5
    for h in range(heads):
        cols = pl.ds(h * hd, hd)
        s = lax.dot_general(q_ref[:, cols], k_ref[:, cols], (((1,), (1,)), ((), ())),
                            preferred_element_type=F32) * scale
        s = s - jnp.max(s, axis=-1, keepdims=True)
        e = jnp.exp(s)
        p = e / jnp.sum(e, axis=-1, keepdims=True)
        o = jnp.dot(p.astype(BF), v_ref[:, cols], preferred_element_type=F32)
        o_ref[:, cols] = o.astype(o_ref.dtype)


def memory_attention(q, kh, vh, seq, tr, col):
    m = q.shape[0]
    _, nm, c = kh.shape
    per = seq // tr
    kv = pl.BlockSpec((None, nm, c), lambda i: (i // per, 0, 0))
    return pl.pallas_call(
        functools.partial(_attn_kernel, heads=MEM_HEADS),
        grid=(m // tr,),
        in_specs=[pl.BlockSpec((tr, c), lambda i: (i, col)), kv, kv],
        out_specs=pl.BlockSpec((tr, c), lambda i: (i, 0)),
        out_shape=jax.ShapeDtypeStruct((m, c), BF),
        compiler_params=_params("parallel"),
        name="memory_attention",
    )(q, kh, vh)


def _merge_kernel(ya, yb, yc, ym, ga, gb, gc, gm, gg, w, x, pg, o_ref):
    g = ya.shape[1]
    acc = None
    for idx, (y_ref, gate) in enumerate(((ya, ga), (yb, gb), (yc, gc), (ym, gm))):
        cols = pl.ds(idx * g, g)
        y = y_ref[...].astype(F32)
        ms = jnp.mean(y * y, axis=-1, keepdims=True)
        yn = y * lax.rsqrt(ms + EPS) * gg[:, cols]
        gt = gate[...]
        yn = yn.astype(BF) * (gt * jax.nn.sigmoid(gt))
        part = jnp.dot(yn, w[cols, :], preferred_element_type=F32)
        acc = part if acc is None else acc + part
    ms = jnp.mean(acc * acc, axis=-1, keepdims=True)
    o_ref[...] = x[...] + acc * lax.rsqrt(ms + EPS) * pg[...]


def merge_project(ya, yb, yc, ym, gate, gate_col, gg, w_out, x, pg, tm):
    m, g = ya.shape
    d = x.shape[1]
    yspec = pl.BlockSpec((tm, g), lambda i: (i, 0))
    gspecs = [pl.BlockSpec((tm, g), lambda i, col=gate_col + k: (i, col)) for k in range(4)]
    const = lambda shape: pl.BlockSpec(shape, lambda i: (0, 0))
    return pl.pallas_call(
        _merge_kernel,
        grid=(m // tm,),
        in_specs=[yspec, yspec, yspec, yspec, *gspecs,
                  const((1, 4 * g)),
                  pl.BlockSpec((4 * g, d), lambda i: (0, 0), pipeline_mode=pl.Buffered(1)),
                  pl.BlockSpec((tm, d), lambda i: (i, 0)),
                  const((1, d))],
        out_specs=pl.BlockSpec((tm, d), lambda i: (i, 0)),
        out_shape=jax.ShapeDtypeStruct((m, d), F32),
        compiler_params=_params("parallel"),
        name="merge_project",
    )(ya, yb, yc, ym, gate, gate, gate, gate, gg, w_out, x, pg)


def _iota(n):
    return np.arange(n, dtype=np.int64)


@functools.lru_cache(maxsize=None)
def _outer_tables(n1, n2, rows_complex):
    n = n1 * n2
    r = _iota(n2)[:, None, None]
    k1 = _iota(n1)[None, :, None]
    nj = n1 // 2 if rows_complex else n1
    j = _iota(nj)[None, None, :]
    er, ei = _cis(k1 * (n2 * j + r), n, -1.0)
    if rows_complex:
        return _bf16_const(_stack_complex(er, ei))
    return _bf16_const(np.concatenate([er, ei], axis=-2))


@functools.lru_cache(maxsize=None)
def _inner_table(n2):
    k = _iota(n2)[:, None]
    j = _iota(n2)[None, :]
    cr, ci = _cis(k * j, n2, -1.0)
    return _bf16_const(_stack_complex(cr, ci))


@functools.lru_cache(maxsize=None)
def _inverse_inner_tables(n1, n2):
    n = n1 * n2
    k1 = _iota(n1)[:, None, None]
    na = _iota(n2)[None, :, None]
    k2 = _iota(n2)[None, None, :]
    ar, ai = _cis(na * (k1 + n1 * k2), n, 1.0)
    return _bf16_const(_stack_complex(ar, ai))


@functools.lru_cache(maxsize=None)
def _inverse_outer_table(n1, n2):
    nb = _iota(n1 // 2)[:, None]
    k1 = _iota(n1)[None, :]
    gr, gi = _cis(nb * k1, n1, 1.0)
    inv = 1.0 / (n1 * n2)
    return _bf16_const(_stack_complex(gr * inv, gi * inv))


@functools.lru_cache(maxsize=None)
def _channel_table(g, scale):
    a = _iota(g)[:, None]
    b = _iota(g)[None, :]
    cg, sg = _cis(a * b, g, 1.0)
    return _bf16_const(np.concatenate([cg, sg], axis=0) * scale)


def _filter_features(positions, seq, n1, n2):
    idx = (n2 * _iota(n1)[None, :] + _iota(n2)[:, None]).reshape(-1)
    pos = positions.astype(F32)
    pos = jnp.concatenate([pos, pos[:1], jnp.flip(pos[1:])])
    pos = pos.reshape(n1, n2).T.reshape(-1)
    t = pos / seq
    bands = jnp.linspace(1e-4, HYENA_BANDS - 1, HYENA_BANDS, dtype=F32)
    w = 2.0 * jnp.pi * pos / seq
    is_f = (idx < seq).astype(F32)
    is_b = (idx > seq).astype(F32)
    feats = jnp.concatenate([t[:, None], jnp.cos(w[:, None] * bands), jnp.sin(w[:, None] * bands),
                             is_f[:, None], is_b[:, None]], axis=-1)
    return jnp.pad(feats, ((0, 0), (0, FEAT_PAD - feats.shape[1])))


def _layer(x2, mem_n, feats, tabs, p, dims):
    bsz, seq, d, g = dims
    m = bsz * seq
    n2 = FFT_INNER
    n1 = 2 * seq // n2
    n1f, n2f = FFT_INNER, seq // FFT_INNER

    proj = input_projection(x2, p["pre_g"], p["w_in"].astype(BF), 1024, g)
    n_in = proj.shape[1]

    y_a = conv_module(proj, p["conv_dw_w"], p["conv_dw_b"], p["conv_ln_g"], p["conv_ln_b"],
                      p["conv_pw_w"], p["conv_pw_b"], seq, 512)

    yr, yi = dft_outer_stage(proj.reshape(bsz, n1f, n2f, n_in), tabs["fnet_outer"], DFT_RESIDUES, 512,
                             "fnet_outer", c=g, col0=2 * g)
    csgw = matmul(tabs["fnet_chan"], p["fnet_w"], 1024, 1024, "fnet_fold")
    y_b = fnet_inner_stage(yr.reshape(bsz, seq, g), yi.reshape(bsz, seq, g), tabs["fnet_inner"],
                           csgw, p["fnet_b"].reshape(1, g), n2f, 16)
    y_b = y_b.reshape(m, g)

    v, x1, x2g = hyena_short_conv(proj, p["hy_short_w"], p["hy_short_b"], seq, 512, 1)
    ffn = p["hy_fw1"].shape[1]
    og = HYENA_ORDER * g
    w1 = jnp.pad(p["hy_fw1"], ((0, FEAT_PAD - p["hy_fw1"].shape[0]), (0, 0)))
    dup = lambda a: jnp.concatenate([a, a], axis=-1)
    w3 = p["hy_fw3"].reshape(ffn, HYENA_ORDER, 2, g)
    w3 = jnp.concatenate([w3[:, :, 0, :].reshape(ffn, og), w3[:, :, 1, :].reshape(ffn, og)], axis=0)
    kur, kui, ss = hyena_filter_taps(
        feats, w1, p["hy_fb1"].reshape(1, ffn), p["hy_freq1"].reshape(1, ffn),
        dup(p["hy_fw2"]), dup(p["hy_fb2"].reshape(1, ffn)), dup(p["hy_freq2"].reshape(1, ffn)),
        w3, p["hy_decay"][:, 0, :].reshape(1, og), p["hy_decay"][:, 1, :].reshape(1, og),
        tabs["filt_outer"], max(1, 512 // n1))
    kfr, kfi = filter_inner_stage(kur.reshape(n2, n1, og), kui.reshape(n2, n1, og), tabs["inner"], ss[0:1],
                                  DFT_RESIDUES, 1024)

    z = v
    for o, gate_o in enumerate((x1, x2g)):
        ur, ui = dft_outer_stage(z.reshape(1, n1, n2, g), tabs["hy_outer"], DFT_RESIDUES, 512, "hyena_outer_dft")
        qr, qi = hyena_mid_stage(ur.reshape(n1 * n2, g), ui.reshape(n1 * n2, g), kfr, kfi,
                                 tabs["inner"], tabs["inv_inner"], o, n2, 8)
        z = hyena_out_stage(qr.reshape(n1, n2, g), qi.reshape(n1, n2, g), tabs["inv_outer"],
                            gate_o.reshape(n1, n2, g), z.reshape(n1, n2, g), p["hy_skip"][o].reshape(1, g),
                            DFT_RESIDUES, 512)
    y_c = z.reshape(m, g)

    nm = mem_n.shape[0] // bsz
    kh = matmul(mem_n, p["mem_wk"], 512, 1024, "mem_k").reshape(bsz, nm, g)
    vh = matmul(mem_n, p["mem_wv"], 512, 1024, "mem_v").reshape(bsz, nm, g)
    y_m = memory_attention(proj, kh, vh, seq, 1024, 6)

    return merge_project(y_a, y_b, y_c, y_m, proj, 7, p["group_g"].reshape(1, 4 * g), p["w_out"].astype(BF),
                         x2, p["post_g"].reshape(1, d), 256)


def kernel(x, mem, positions, mem_norm_g, pre_norm_g, post_norm_g, w_in, conv_dw_w, conv_dw_b, conv_ln_g, conv_ln_b, conv_pw_w, conv_pw_b, fnet_w, fnet_b, hy_short_w, hy_short_b, hy_fw1, hy_fb1, hy_freq1, hy_fw2, hy_fb2, hy_freq2, hy_fw3, hy_decay, hy_skip, mem_wk, mem_wv, group_norm_g, w_out):
    bsz, seq, d = x.shape
    g = conv_pw_w.shape[-1]
    depth = w_in.shape[0]
    assert bsz == 2, "the long convolution packs exactly two batch elements as one complex sequence"
    assert seq % (FFT_INNER * DFT_RESIDUES) == 0
    n2 = FFT_INNER
    n1 = 2 * seq // n2
    n1f, n2f = FFT_INNER, seq // FFT_INNER

    tabs = {
        "fnet_outer": _outer_tables(n1f, n2f, False),
        "fnet_inner": _inner_table(n2f),
        "fnet_chan": _channel_table(g, 1.0 / math.sqrt(seq * g)),
        "hy_outer": _outer_tables(n1, n2, True),
        "filt_outer": _outer_tables(n1, n2, False),
        "inner": _inner_table(n2),
        "inv_inner": _inverse_inner_tables(n1, n2),
        "inv_outer": _inverse_outer_table(n1, n2),
    }
    tabs = {name: jnp.asarray(table) for name, table in tabs.items()}
    feats = _filter_features(positions, seq, n1, n2)
    mem_n = rmsnorm_bf16(mem.reshape(-1, d), mem_norm_g, min(512, mem.shape[0] * mem.shape[1]))

    x2 = x.reshape(bsz * seq, d)
    for l in range(depth):
        p = {
            "pre_g": pre_norm_g[l], "post_g": post_norm_g[l], "w_in": w_in[l],
            "conv_dw_w": conv_dw_w[l], "conv_dw_b": conv_dw_b[l], "conv_ln_g": conv_ln_g[l],
            "conv_ln_b": conv_ln_b[l], "conv_pw_w": conv_pw_w[l], "conv_pw_b": conv_pw_b[l],
            "fnet_w": fnet_w[l], "fnet_b": fnet_b[l], "hy_short_w": hy_short_w[l], "hy_short_b": hy_short_b[l],
            "hy_fw1": hy_fw1[l], "hy_fb1": hy_fb1[l], "hy_freq1": hy_freq1[l], "hy_fw2": hy_fw2[l],
            "hy_fb2": hy_fb2[l], "hy_freq2": hy_freq2[l], "hy_fw3": hy_fw3[l], "hy_decay": hy_decay[l],
            "hy_skip": hy_skip[l], "mem_wk": mem_wk[l], "mem_wv": mem_wv[l], "group_g": group_norm_g[l],
            "w_out": w_out[l],
        }
        x2 = _layer(x2, mem_n, feats, tabs, p, (bsz, seq, d, g))
    return x2.reshape(bsz, seq, d)
```

```python
import functools
import math

import jax
import jax.numpy as jnp
import numpy as np
from jax import lax
from jax.experimental import pallas as pl
from jax.experimental.pallas import tpu as pltpu

BF = jnp.bfloat16
F32 = jnp.float32
EPS = 1e-6

CONV_WIDTH = 31
CONV_HALF = CONV_WIDTH // 2
HALO = 16
SUBLANES, LANES = 8, 128
HYENA_BANDS = 16
HYENA_ORDER = 2
MEM_HEADS = 4
FFT_INNER = 128
FEAT_PAD = 128
FWD_COL = 2 * HYENA_BANDS + 1
BWD_COL = 2 * HYENA_BANDS + 2
DFT_RESIDUES = 16
VMEM_LIMIT = 56 * 1024 * 1024


def _params(*sem):
    return pltpu.CompilerParams(dimension_semantics=sem, vmem_limit_bytes=VMEM_LIMIT)


def _rmsnorm_kernel(x_ref, g_ref, o_ref):
    x = x_ref[...]
    ms = jnp.mean(x * x, axis=-1, keepdims=True)
    o_ref[...] = (x * lax.rsqrt(ms + EPS) * g_ref[...]).astype(o_ref.dtype)


def rmsnorm_bf16(x, g, tm):
    m, d = x.shape
    return pl.pallas_call(
        _rmsnorm_kernel,
        grid=(m // tm,),
        in_specs=[pl.BlockSpec((tm, d), lambda i: (i, 0)),
                  pl.BlockSpec((1, d), lambda i: (0, 0))],
        out_specs=pl.BlockSpec((tm, d), lambda i: (i, 0)),
        out_shape=jax.ShapeDtypeStruct((m, d), BF),
        compiler_params=_params("parallel"),
        name="rmsnorm",
    )(x, g.reshape(1, d))


def _mm_kernel(a_ref, w_ref, o_ref):
    o_ref[...] = jnp.dot(a_ref[...], w_ref[...], preferred_element_type=F32).astype(o_ref.dtype)


def matmul(a, w, tm, tn, name, col0=0, n=None):
    m, k = a.shape
    n = w.shape[1] if n is None else n
    tm, tn = min(tm, m), min(tn, n)
    joff = col0 // tn
    return pl.pallas_call(
        _mm_kernel,
        grid=(m // tm, n // tn),
        in_specs=[pl.BlockSpec((tm, k), lambda i, j: (i, 0)),
                  pl.BlockSpec((k, tn), lambda i, j: (0, j + joff))],
        out_specs=pl.BlockSpec((tm, tn), lambda i, j: (i, j)),
        out_shape=jax.ShapeDtypeStruct((m, n), BF),
        compiler_params=_params("parallel", "arbitrary"),
        name=name,
    )(a, w)


def _proj_kernel(x_ref, g_ref, w_ref, o_ref, h_s):
    @pl.when(pl.program_id(1) == 0)
    def _():
        x = x_ref[...]
        ms = jnp.mean(x * x, axis=-1, keepdims=True)
        h_s[...] = (x * lax.rsqrt(ms + EPS) * g_ref[...]).astype(h_s.dtype)

    o_ref[...] = jnp.dot(h_s[...], w_ref[...], preferred_element_type=F32).astype(o_ref.dtype)


def input_projection(x, g, w, layer, tm, tn):
    m, d = x.shape
    n = w.shape[2]
    return pl.pallas_call(
        _proj_kernel,
        grid=(m // tm, n // tn),
        in_specs=[pl.BlockSpec((tm, d), lambda i, j: (i, 0)),
                  pl.BlockSpec((1, d), lambda i, j: (0, 0)),
                  pl.BlockSpec((None, d, tn), lambda i, j: (layer, 0, j))],
        out_specs=pl.BlockSpec((tm, tn), lambda i, j: (i, j)),
        out_shape=jax.ShapeDtypeStruct((m, n), BF),
        scratch_shapes=[pltpu.VMEM((tm, d), BF)],
        compiler_params=_params("parallel", "arbitrary"),
        name="input_projection",
    )(x, g.reshape(1, d), w)


def _halo_specs(tr, width, col):
    per = tr // HALO

    def cur(i):
        return (i, col)

    def prev(i):
        return (jnp.maximum(i * per - 1, 0), col)

    def make_next(nblocks):
        def nxt(i):
            return (jnp.minimum((i + 1) * per, nblocks - 1), col)
        return nxt

    return cur, prev, make_next


def _seq_edges(i, tr, seq):
    tiles = seq // tr
    pos = i % tiles
    return pos == 0, pos == tiles - 1


def _conva_kernel(vc, gc, vp, gp, vn, gn, dww, dwb, lng, lnb, pww, pwb, o_ref, upad, cbuf, wbc, *, seq):
    tr, c = vc.shape
    first, last = _seq_edges(pl.program_id(0), tr, seq)

    def glu(v, g):
        return v[...].astype(F32) * jax.nn.sigmoid(g[...].astype(F32))

    upad[pl.ds(HALO, tr), :] = glu(vc, gc)
    upad[pl.ds(0, HALO), :] = jnp.where(first, 0.0, glu(vp, gp))
    upad[pl.ds(HALO + tr, HALO), :] = jnp.where(last, 0.0, glu(vn, gn))

    for j in range(CONV_WIDTH):
        wbc[pl.ds(SUBLANES * j, SUBLANES), :] = jnp.broadcast_to(dww[pl.ds(j, 1), :], (SUBLANES, c))
    row = lax.broadcasted_iota(jnp.int32, (SUBLANES, LANES), 0)
    for ct in range(c // LANES):
        lanes = pl.ds(ct * LANES, LANES)

        def residue_sums(b, lanes=lanes):
            u = [upad[pl.ds(pl.multiple_of((b + a) * SUBLANES, SUBLANES), SUBLANES), lanes] for a in range(4)]
            sums = []
            for s in range(SUBLANES):
                acc = None
                for a in range(4):
                    j = SUBLANES * a + s - 1
                    if 0 <= j < CONV_WIDTH:
                        term = u[a] * wbc[pl.ds(SUBLANES * j, SUBLANES), lanes]
                        acc = term if acc is None else acc + term
                sums.append(acc)
            return tuple(sums)

        bias = jnp.broadcast_to(dwb[:, lanes], (SUBLANES, LANES))

        def body(b, prev, lanes=lanes, bias=bias, residue_sums=residue_sums):
            cur = residue_sums(b)
            terms = [bias + prev[0]]
            for s in range(1, SUBLANES):
                terms.append(pltpu.roll(jnp.where(row >= s, prev[s], cur[s]), SUBLANES - s, axis=0))
            while len(terms) > 1:
                terms = [terms[i] + terms[i + 1] for i in range(0, len(terms), 2)]
            out = terms[0]
            cbuf[pl.ds(pl.multiple_of((b - 1) * SUBLANES, SUBLANES), SUBLANES), lanes] = out
            return cur

        lax.fori_loop(1, tr // SUBLANES + 1, body, residue_sums(0), unroll=4)

    u = cbuf[...]
    mu = jnp.mean(u, axis=-1, keepdims=True)
    d = u - mu
    var = jnp.mean(d * d, axis=-1, keepdims=True)
    y = d * lax.rsqrt(var + EPS) * lng[...] + lnb[...]
    y = y * jax.nn.sigmoid(y)
    out = jnp.dot(y.astype(BF), pww[...], preferred_element_type=F32) + pwb[...]
    o_ref[...] = out.astype(o_ref.dtype)


def conv_module(a, dww, dwb, lng, lnb, pww, pwb, seq, tr):
    m = a.shape[0]
    c = pww.shape[0]
    cur, prev, make_next = _halo_specs(tr, c, 0)
    cur_g, prev_g, make_next_g = _halo_specs(tr, c, 1)
    nh = m // HALO
    row = lambda i: (0, 0)
    return pl.pallas_call(
        functools.partial(_conva_kernel, seq=seq),
        grid=(m // tr,),
        in_specs=[pl.BlockSpec((tr, c), cur), pl.BlockSpec((tr, c), cur_g),
                  pl.BlockSpec((HALO, c), prev), pl.BlockSpec((HALO, c), prev_g),
                  pl.BlockSpec((HALO, c), make_next(nh)), pl.BlockSpec((HALO, c), make_next_g(nh)),
                  pl.BlockSpec((CONV_WIDTH, c), row), pl.BlockSpec((1, c), row),
                  pl.BlockSpec((1, c), row), pl.BlockSpec((1, c), row),
                  pl.BlockSpec((c, c), row), pl.BlockSpec((1, c), row)],
        out_specs=pl.BlockSpec((tr, c), lambda i: (i, 0)),
        out_shape=jax.ShapeDtypeStruct((m, c), BF),
        scratch_shapes=[pltpu.VMEM((tr + 2 * HALO, c), F32), pltpu.VMEM((tr, c), F32),
                        pltpu.VMEM((CONV_WIDTH * SUBLANES, c), F32)],
        compiler_params=_params("parallel"),
        name="conv_module",
    )(a, a, a, a, a, a, dww, dwb.reshape(1, c), lng.reshape(1, c), lnb.reshape(1, c),
      pww.astype(BF), pwb.reshape(1, c))


def _short_kernel(hc, hp, hn, w, b, v_ref, x1_ref, x2_ref, *, seq):
    tr, c3 = hc.shape
    c = c3 // 3
    first, last = _seq_edges(pl.program_id(0), tr, seq)
    edge = lax.broadcasted_iota(jnp.int32, (HALO, c), 0)
    for g, o_ref in enumerate((v_ref, x1_ref, x2_ref)):
        cols = pl.ds(g * c, c)
        w0, w1, w2, bias = w[pl.ds(0, 1), cols], w[pl.ds(1, 1), cols], w[pl.ds(2, 1), cols], b[:, cols]
        cur = hc[:, cols].astype(F32)
        before = pltpu.roll(cur, 1, axis=0)
        after = pltpu.roll(cur, tr - 1, axis=0)
        o_ref[...] = (before * w0 + cur * w1 + after * w2 + bias).astype(o_ref.dtype)

        prev_row = jnp.where(first, 0.0, hp[pl.ds(HALO - 1, 1), cols].astype(F32))
        next_row = jnp.where(last, 0.0, hn[pl.ds(0, 1), cols].astype(F32))
        head = jnp.where(edge == 0, prev_row, before[:HALO])
        tail = jnp.where(edge == HALO - 1, next_row, after[tr - HALO:])
        o_ref[pl.ds(0, HALO), :] = (head * w0 + cur[:HALO] * w1 + after[:HALO] * w2 + bias).astype(o_ref.dtype)
        o_ref[pl.ds(tr - HALO, HALO), :] = (before[tr - HALO:] * w0 + cur[tr - HALO:] * w1 + tail * w2
                                            + bias).astype(o_ref.dtype)


def hyena_short_conv(hy, w, b, seq, tr, col):
    m = hy.shape[0]
    c3 = w.shape[1]
    c = c3 // 3
    cur, prev, make_next = _halo_specs(tr, c3, col)
    out = jax.ShapeDtypeStruct((m, c), BF)
    ospec = pl.BlockSpec((tr, c), lambda i: (i, 0))
    return pl.pallas_call(
        functools.partial(_short_kernel, seq=seq),
        grid=(m // tr,),
        in_specs=[pl.BlockSpec((tr, c3), cur), pl.BlockSpec((HALO, c3), prev),
                  pl.BlockSpec((HALO, c3), make_next(m // HALO)),
                  pl.BlockSpec((3, c3), lambda i: (0, 0)), pl.BlockSpec((1, c3), lambda i: (0, 0))],
        out_specs=[ospec, ospec, ospec],
        out_shape=[out, out, out],
        compiler_params=_params("parallel"),
        name="hyena_short_conv",
    )(hy, hy, hy, w, b.reshape(1, c3))


def _cis(p, n, sign):
    ang = (2.0 * np.pi / n) * (p % n)
    return np.cos(ang), sign * np.sin(ang)


def _stack_complex(mr, mi):
    return np.concatenate([np.concatenate([mr, -mi], axis=-1),
                           np.concatenate([mi, mr], axis=-1)], axis=-2)


def _bf16_const(a):
    return a.astype(jnp.bfloat16)


def _outer_stage_kernel(x_ref, t_ref, or_ref, oi_ref, yr_s, yi_s, *, nb):
    xt = pltpu.einshape("nrc->rnc", x_ref[...])
    for r in range(nb):
        y = jnp.dot(t_ref[r], xt[r], preferred_element_type=F32)
        h = y.shape[0] // 2
        yr_s[r] = y[:h].astype(BF)
        yi_s[r] = y[h:].astype(BF)
    or_ref[...] = pltpu.einshape("rnc->nrc", yr_s[...])
    oi_ref[...] = pltpu.einshape("rnc->nrc", yi_s[...])


def dft_outer_stage(x, table, nb, tc, name, c=None, col0=0):
    bsz, rows, n_inner, _ = x.shape
    c = x.shape[3] if c is None else c
    _, two_k, _ = table.shape
    kout = two_k // 2
    tc = min(tc, c)
    coff = col0 // tc
    out = jax.ShapeDtypeStruct((bsz, kout, n_inner, c), BF)
    ospec = pl.BlockSpec((None, kout, nb, tc), lambda b, j, cc: (b, 0, j, cc))
    return pl.pallas_call(
        functools.partial(_outer_stage_kernel, nb=nb),
        grid=(bsz, n_inner // nb, c // tc),
        in_specs=[pl.BlockSpec((None, rows, nb, tc), lambda b, j, cc: (b, 0, j, cc + coff)),
                  pl.BlockSpec((nb, two_k, rows), lambda b, j, cc: (j, 0, 0))],
        out_specs=[ospec, ospec],
        out_shape=[out, out],
        scratch_shapes=[pltpu.VMEM((nb, kout, tc), BF), pltpu.VMEM((nb, kout, tc), BF)],
        compiler_params=_params("parallel", "arbitrary", "arbitrary"),
        name=name,
    )(x, table)


def _filter_inner_kernel(ur, ui, d, ss, kr_ref, ki_ref, *, kb, n2):
    scale = lax.rsqrt(ss[...] + EPS)
    urt = pltpu.einshape("nkc->knc", ur[...])
    uit = pltpu.einshape("nkc->knc", ui[...])
    for k in range(kb):
        rows = pl.ds(k * n2, n2)
        cat = jnp.concatenate([urt[k], uit[k]], axis=0)
        x = jnp.dot(d[...], cat, preferred_element_type=F32) * scale
        kr_ref[rows, :] = x[:n2].astype(kr_ref.dtype)
        ki_ref[rows, :] = x[n2:].astype(ki_ref.dtype)


def filter_inner_stage(ur, ui, d, ss, kb, tc):
    n2, n1, c = ur.shape
    tc = min(tc, c)
    ispec = pl.BlockSpec((n2, kb, tc), lambda i, cc: (0, i, cc))
    ospec = pl.BlockSpec((kb * n2, tc), lambda i, cc: (i, cc))
    out = jax.ShapeDtypeStruct((n1 * n2, c), BF)
    return pl.pallas_call(
        functools.partial(_filter_inner_kernel, kb=kb, n2=n2),
        grid=(n1 // kb, c // tc),
        in_specs=[ispec, ispec, pl.BlockSpec((2 * n2, 2 * n2), lambda i, cc: (0, 0)),
                  pl.BlockSpec((1, tc), lambda i, cc: (0, cc))],
        out_specs=[ospec, ospec],
        out_shape=[out, out],
        compiler_params=_params("parallel", "arbitrary"),
        name="filter_inner_dft",
    )(ur, ui, d, ss)


def _hyena_mid_kernel(ur, ui, kr, ki, d, minv, qr_ref, qi_ref, *, kb, n2):
    for k in range(kb):
        rows = pl.ds(k * n2, n2)
        cat = jnp.concatenate([ur[rows, :], ui[rows, :]], axis=0)
        x = jnp.dot(d[...], cat, preferred_element_type=F32)
        xr, xi = x[:n2], x[n2:]
        fr, fi = kr[rows, :].astype(F32), ki[rows, :].astype(F32)
        pr = xr * fr - xi * fi
        pi = xr * fi + xi * fr
        catp = jnp.concatenate([pr, pi], axis=0).astype(BF)
        q = jnp.dot(minv[k], catp, preferred_element_type=F32)
        qr_ref[rows, :] = q[:n2].astype(qr_ref.dtype)
        qi_ref[rows, :] = q[n2:].astype(qi_ref.dtype)


def hyena_mid_stage(ur, ui, kfr, kfi, d, minv, order, n2, kb):
    n, c = ur.shape
    n1 = n // n2
    kb = min(kb, n1)
    spec = pl.BlockSpec((kb * n2, c), lambda i: (i, 0))
    kspec = pl.BlockSpec((kb * n2, c), lambda i: (i, order))
    out = jax.ShapeDtypeStruct((n, c), BF)
    return pl.pallas_call(
        functools.partial(_hyena_mid_kernel, kb=kb, n2=n2),
        grid=(n1 // kb,),
        in_specs=[spec, spec, kspec, kspec,
                  pl.BlockSpec((2 * n2, 2 * n2), lambda i: (0, 0)),
                  pl.BlockSpec((kb, 2 * n2, 2 * n2), lambda i: (i, 0, 0))],
        out_specs=[spec, spec],
        out_shape=[out, out],
        compiler_params=_params("parallel"),
        name="hyena_mid",
    )(ur, ui, kfr, kfi, d, minv)


def _hyena_out_kernel(qr, qi, g, xo, z, skip, o_ref, y_s, *, nb):
    qrt = pltpu.einshape("nrc->rnc", qr[...])
    qit = pltpu.einshape("nrc->rnc", qi[...])
    for r in range(nb):
        cat = jnp.concatenate([qrt[r], qit[r]], axis=0)
        y_s[r] = jnp.dot(g[...], cat, preferred_element_type=F32)
    y = pltpu.einshape("rnc->nrc", y_s[...])
    zf = z[...].astype(F32)
    o_ref[...] = (xo[...].astype(F32) * (y + skip[...] * zf)).astype(o_ref.dtype)


def hyena_out_stage(qr, qi, g, xo, z, skip, nb, tc):
    rows, n2, c = qr.shape
    tc = min(tc, c)
    spec = pl.BlockSpec((rows, nb, tc), lambda j, cc: (0, j, cc))
    return pl.pallas_call(
        functools.partial(_hyena_out_kernel, nb=nb),
        grid=(n2 // nb, c // tc),
        in_specs=[spec, spec, pl.BlockSpec((rows, 2 * rows), lambda j, cc: (0, 0)), spec, spec,
                  pl.BlockSpec((1, tc), lambda j, cc: (0, cc))],
        out_specs=spec,
        out_shape=jax.ShapeDtypeStruct((rows, n2, c), BF),
        scratch_shapes=[pltpu.VMEM((nb, rows, tc), F32)],
        compiler_params=_params("parallel", "arbitrary"),
        name="hyena_out",
    )(qr, qi, g, xo, z, skip)


def _dot_bf16x3(a, b):
    ah = a.astype(BF)
    bh = b.astype(BF)
    al = (a - ah.astype(F32)).astype(BF)
    bl = (b - bh.astype(F32)).astype(BF)
    return jnp.dot(jnp.concatenate([ah, al, ah], axis=1), jnp.concatenate([bh, bh, bl], axis=0),
                   preferred_element_type=F32)


def _filter_kernel(ft, w1, b1, f1, w2, b2, f2, w3, decf, decb, tab, ur_ref, ui_ref, ss_ref, *, n1):
    hi = lax.Precision.HIGHEST
    x = ft[...]
    ffn2 = w2.shape[1]
    h = jnp.sin(f1[...] * (jnp.dot(x, w1[...], precision=hi, preferred_element_type=F32) + b1[...]))
    h = jnp.sin(f2[...] * (jnp.dot(h, w2[...], precision=hi, preferred_element_type=F32) + b2[...]))
    t = x[:, 0:1]
    is_f = x[:, FWD_COL:FWD_COL + 1]
    is_b = x[:, BWD_COL:BWD_COL + 1]
    lane = lax.broadcasted_iota(jnp.int32, h.shape, 1)
    h = h * jnp.where(lane < ffn2 // 2, is_f, is_b)
    h = _dot_bf16x3(h, w3[...])
    dec = jnp.where(is_f > 0.0, jnp.abs(decf[...]), jnp.abs(decb[...]))
    k = h * jnp.exp(-t * dec)
    kb = k.astype(BF)
    for r in range(tab.shape[0]):
        rows = pl.ds(r * n1, n1)
        u = jnp.dot(tab[r], kb[r * n1:(r + 1) * n1], preferred_element_type=F32)
        ur_ref[rows, :] = u[:n1].astype(ur_ref.dtype)
        ui_ref[rows, :] = u[n1:].astype(ui_ref.dtype)
    part = jnp.sum(k * k, axis=0, keepdims=True)

    @pl.when(pl.program_id(0) == 0)
    def _():
        ss_ref[...] = jnp.zeros_like(ss_ref)

    ss_ref[...] += jnp.broadcast_to(part, ss_ref.shape)


def hyena_filter_taps(feats, w1, b1, f1, w2, b2, f2, w3, decf, decb, table, nb):
    n, fp = feats.shape
    ffn = w1.shape[1]
    ffn2 = w2.shape[1]
    oc = w3.shape[-1]
    _, two_n1, n1 = table.shape
    tr = nb * n1
    small = lambda shape: pl.BlockSpec(shape, lambda i: (0,) * len(shape))
    ospec = pl.BlockSpec((tr, oc), lambda i: (i, 0))
    out = jax.ShapeDtypeStruct((n, oc), BF)
    return pl.pallas_call(
        functools.partial(_filter_kernel, n1=n1),
        grid=(n // tr,),
        in_specs=[pl.BlockSpec((tr, fp), lambda i: (i, 0)),
                  small((fp, ffn)), small((1, ffn)), small((1, ffn)),
                  small((ffn, ffn2)), small((1, ffn2)), small((1, ffn2)),
                  small((ffn2, oc)), small((1, oc)), small((1, oc)),
                  pl.BlockSpec((nb, two_n1, n1), lambda i: (i, 0, 0))],
        out_specs=[ospec, ospec, pl.BlockSpec((8, oc), lambda i: (0, 0))],
        out_shape=[out, out, jax.ShapeDtypeStruct((8, oc), F32)],
        compiler_params=_params("arbitrary"),
        name="hyena_filter_taps",
    )(feats, w1, b1, f1, w2, b2, f2, w3, decf, decb, table)


def _fnet_kernel(yr, yi, d2, csgw, b, o_ref, xcat, o_s, *, kb, n2, c):
    for k in range(kb):
        rows = pl.ds(k * n2, n2)
        cat = jnp.concatenate([yr[rows, :], yi[rows, :]], axis=0)
        x = jnp.dot(d2[...], cat, preferred_element_type=F32)
        xcat[rows, pl.ds(0, c)] = x[:n2].astype(BF)
        xcat[rows, pl.ds(c, c)] = x[n2:].astype(BF)
    out = jnp.dot(xcat[...], csgw[...], preferred_element_type=F32) + b[...]
    for k in range(kb):
        o_s[k] = out[k * n2:(k + 1) * n2].astype(o_s.dtype)
    o_ref[...] = pltpu.einshape("kmc->mkc", o_s[...])


def fnet_inner_stage(yr, yi, d2, csgw, b, n2, kb):
    bsz, l, c = yr.shape
    n1 = l // n2
    kb = min(kb, n1)
    spec = pl.BlockSpec((None, kb * n2, c), lambda bb, i: (bb, i, 0))
    const = lambda shape: pl.BlockSpec(shape, lambda bb, i: (0,) * len(shape))
    return pl.pallas_call(
        functools.partial(_fnet_kernel, kb=kb, n2=n2, c=c),
        grid=(bsz, n1 // kb),
        in_specs=[spec, spec, const((2 * n2, 2 * n2)), const((2 * c, c)), const((1, c))],
        out_specs=pl.BlockSpec((None, n2, kb, c), lambda bb, i: (bb, 0, i, 0)),
        out_shape=jax.ShapeDtypeStruct((bsz, n2, n1, c), BF),
        scratch_shapes=[pltpu.VMEM((kb * n2, 2 * c), BF), pltpu.VMEM((kb, n2, c), BF)],
        compiler_params=_params("parallel", "arbitrary"),
        name="fnet_inner",
    )(yr, yi, d2, csgw, b)


def _attn_kernel(q_ref, k_ref, v_ref, o_ref, *, heads):
    c = q_ref.shape[1]
    hd = c // heads
    scale = hd ** -0.5
    for h in range(heads):
        cols = pl.ds(h * hd, hd)
        s = lax.dot_general(q_ref[:, cols], k_ref[:, cols], (((1,), (1,)), ((), ())),
                            preferred_element_type=F32) * scale
        s = s - jnp.max(s, axis=-1, keepdims=True)
        e = jnp.exp(s)
        p = e / jnp.sum(e, axis=-1, keepdims=True)
        o = jnp.dot(p.astype(BF), v_ref[:, cols], preferred_element_type=F32)
        o_ref[:, cols] = o.astype(o_ref.dtype)


def memory_attention(q, kh, vh, seq, tr, col):
    m = q.shape[0]
    _, nm, c = kh.shape
    per = seq // tr
    kv = pl.BlockSpec((None, nm, c), lambda i: (i // per, 0, 0))
    return pl.pallas_call(
        functools.partial(_attn_kernel, heads=MEM_HEADS),
        grid=(m // tr,),
        in_specs=[pl.BlockSpec((tr, c), lambda i: (i, col)), kv, kv],
        out_specs=pl.BlockSpec((tr, c), lambda i: (i, 0)),
        out_shape=jax.ShapeDtypeStruct((m, c), BF),
        compiler_params=_params("parallel"),
        name="memory_attention",
    )(q, kh, vh)


def _merge_kernel(ya, yb, yc, ym, ga, gb, gc, gm, gg, w, x, pg, o_ref):
    g = ya.shape[1]
    acc = None
    for idx, (y_ref, gate) in enumerate(((ya, ga), (yb, gb), (yc, gc), (ym, gm))):
        cols = pl.ds(idx * g, g)
        y = y_ref[...].astype(F32)
        ms = jnp.mean(y * y, axis=-1, keepdims=True)
        yn = y * lax.rsqrt(ms + EPS) * gg[:, cols]
        gt = gate[...]
        yn = yn.astype(BF) * (gt * jax.nn.sigmoid(gt))
        part = jnp.dot(yn, w[cols, :], preferred_element_type=F32)
        acc = part if acc is None else acc + part
    ms = jnp.mean(acc * acc, axis=-1, keepdims=True)
    o_ref[...] = x[...] + acc * lax.rsqrt(ms + EPS) * pg[...]


def merge_project(ya, yb, yc, ym, gate, gate_col, gg, w_out, layer, x, pg, tm):
    m, g = ya.shape
    d = x.shape[1]
    yspec = pl.BlockSpec((tm, g), lambda i: (i, 0))
    gspecs = [pl.BlockSpec((tm, g), lambda i, col=gate_col + k: (i, col)) for k in range(4)]
    const = lambda shape: pl.BlockSpec(shape, lambda i: (0, 0))
    return pl.pallas_call(
        _merge_kernel,
        grid=(m // tm,),
        in_specs=[yspec, yspec, yspec, yspec, *gspecs,
                  const((1, 4 * g)),
                  pl.BlockSpec((None, 4 * g, d), lambda i: (layer, 0, 0), pipeline_mode=pl.Buffered(1)),
                  pl.BlockSpec((tm, d), lambda i: (i, 0)),
                  const((1, d))],
        out_specs=pl.BlockSpec((tm, d), lambda i: (i, 0)),
        out_shape=jax.ShapeDtypeStruct((m, d), F32),
        compiler_params=_params("parallel"),
        name="merge_project",
    )(ya, yb, yc, ym, gate, gate, gate, gate, gg, w_out, x, pg)


def _iota(n):
    return np.arange(n, dtype=np.int64)


@functools.lru_cache(maxsize=None)
def _outer_tables(n1, n2, rows_complex):
    n = n1 * n2
    r = _iota(n2)[:, None, None]
    k1 = _iota(n1)[None, :, None]
    nj = n1 // 2 if rows_complex else n1
    j = _iota(nj)[None, None, :]
    er, ei = _cis(k1 * (n2 * j + r), n, -1.0)
    if rows_complex:
        return _bf16_const(_stack_complex(er, ei))
    return _bf16_const(np.concatenate([er, ei], axis=-2))


@functools.lru_cache(maxsize=None)
def _inner_table(n2):
    k = _iota(n2)[:, None]
    j = _iota(n2)[None, :]
    cr, ci = _cis(k * j, n2, -1.0)
    return _bf16_const(_stack_complex(cr, ci))


@functools.lru_cache(maxsize=None)
def _inverse_inner_tables(n1, n2):
    n = n1 * n2
    k1 = _iota(n1)[:, None, None]
    na = _iota(n2)[None, :, None]
    k2 = _iota(n2)[None, None, :]
    ar, ai = _cis(na * (k1 + n1 * k2), n, 1.0)
    return _bf16_const(_stack_complex(ar, ai))


@functools.lru_cache(maxsize=None)
def _inverse_outer_table(n1, n2):
    nb = _iota(n1 // 2)[:, None]
    k1 = _iota(n1)[None, :]
    gr, gi = _cis(nb * k1, n1, 1.0)
    inv = 1.0 / (n1 * n2)
    return _bf16_const(_stack_complex(gr * inv, gi * inv))


@functools.lru_cache(maxsize=None)
def _channel_table(g, scale):
    a = _iota(g)[:, None]
    b = _iota(g)[None, :]
    cg, sg = _cis(a * b, g, 1.0)
    return _bf16_const(np.concatenate([cg, sg], axis=0) * scale)


def _filter_features(positions, seq, n1, n2):
    idx = (n2 * _iota(n1)[None, :] + _iota(n2)[:, None]).reshape(-1)
    pos = positions.astype(F32)
    pos = jnp.concatenate([pos, pos[:1], jnp.flip(pos[1:])])
    pos = pos.reshape(n1, n2).T.reshape(-1)
    t = pos / seq
    bands = jnp.linspace(1e-4, HYENA_BANDS - 1, HYENA_BANDS, dtype=F32)
    w = 2.0 * jnp.pi * pos / seq
    is_f = (idx < seq).astype(F32)
    is_b = (idx > seq).astype(F32)
    feats = jnp.concatenate([t[:, None], jnp.cos(w[:, None] * bands), jnp.sin(w[:, None] * bands),
                             is_f[:, None], is_b[:, None]], axis=-1)
    return jnp.pad(feats, ((0, 0), (0, FEAT_PAD - feats.shape[1])))


def _layer(x2, mem_n, feats, tabs, p, dims):
    bsz, seq, d, g = dims
    m = bsz * seq
    n2 = FFT_INNER
    n1 = 2 * seq // n2
    n1f, n2f = FFT_INNER, seq // FFT_INNER

    proj = input_projection(x2, p["pre_g"], p["w_in_all"], p["layer"], 1024, g)
    n_in = proj.shape[1]

    y_a = conv_module(proj, p["conv_dw_w"], p["conv_dw_b"], p["conv_ln_g"], p["conv_ln_b"],
                      p["conv_pw_w"], p["conv_pw_b"], seq, 512)

    yr, yi = dft_outer_stage(proj.reshape(bsz, n1f, n2f, n_in), tabs["fnet_outer"], DFT_RESIDUES, 512,
                             "fnet_outer", c=g, col0=2 * g)
    csgw = matmul(tabs["fnet_chan"], p["fnet_w"].astype(BF), 1024, 1024, "fnet_fold")
    y_b = fnet_inner_stage(yr.reshape(bsz, seq, g), yi.reshape(bsz, seq, g), tabs["fnet_inner"],
                           csgw, p["fnet_b"].reshape(1, g), n2f, 16)
    y_b = y_b.reshape(m, g)

    v, x1, x2g = hyena_short_conv(proj, p["hy_short_w"], p["hy_short_b"], seq, 512, 1)
    ffn = p["hy_fw1"].shape[1]
    og = HYENA_ORDER * g
    w1 = jnp.pad(p["hy_fw1"], ((0, FEAT_PAD - p["hy_fw1"].shape[0]), (0, 0)))
    dup = lambda a: jnp.concatenate([a, a], axis=-1)
    w3 = p["hy_fw3"].reshape(ffn, HYENA_ORDER, 2, g)
    w3 = jnp.concatenate([w3[:, :, 0, :].reshape(ffn, og), w3[:, :, 1, :].reshape(ffn, og)], axis=0)
    kur, kui, ss = hyena_filter_taps(
        feats, w1, p["hy_fb1"].reshape(1, ffn), p["hy_freq1"].reshape(1, ffn),
        dup(p["hy_fw2"]), dup(p["hy_fb2"].reshape(1, ffn)), dup(p["hy_freq2"].reshape(1, ffn)),
        w3, p["hy_decay"][:, 0, :].reshape(1, og), p["hy_decay"][:, 1, :].reshape(1, og),
        tabs["filt_outer"], max(1, 512 // n1))
    kfr, kfi = filter_inner_stage(kur.reshape(n2, n1, og), kui.reshape(n2, n1, og), tabs["inner"], ss[0:1],
                                  DFT_RESIDUES, 1024)

    z = v
    for o, gate_o in enumerate((x1, x2g)):
        ur, ui = dft_outer_stage(z.reshape(1, n1, n2, g), tabs["hy_outer"], DFT_RESIDUES, 512, "hyena_outer_dft")
        qr, qi = hyena_mid_stage(ur.reshape(n1 * n2, g), ui.reshape(n1 * n2, g), kfr, kfi,
                                 tabs["inner"], tabs["inv_inner"], o, n2, 8)
        z = hyena_out_stage(qr.reshape(n1, n2, g), qi.reshape(n1, n2, g), tabs["inv_outer"],
                            gate_o.reshape(n1, n2, g), z.reshape(n1, n2, g), p["hy_skip"][o].reshape(1, g),
                            DFT_RESIDUES, 512)
    y_c = z.reshape(m, g)

    nm = mem_n.shape[0] // bsz
    kh = matmul(mem_n, p["mem_wk"].astype(BF), 512, 1024, "mem_k").reshape(bsz, nm, g)
    vh = matmul(mem_n, p["mem_wv"].astype(BF), 512, 1024, "mem_v").reshape(bsz, nm, g)
    y_m = memory_attention(proj, kh, vh, seq, 1024, 6)

    return merge_project(y_a, y_b, y_c, y_m, proj, 7, p["group_g"].reshape(1, 4 * g), p["w_out_all"], p["layer"],
                         x2, p["post_g"].reshape(1, d), 256)


def kernel(x, mem, positions, mem_norm_g, pre_norm_g, post_norm_g, w_in, conv_dw_w, conv_dw_b, conv_ln_g, conv_ln_b, conv_pw_w, conv_pw_b, fnet_w, fnet_b, hy_short_w, hy_short_b, hy_fw1, hy_fb1, hy_freq1, hy_fw2, hy_fb2, hy_freq2, hy_fw3, hy_decay, hy_skip, mem_wk, mem_wv, group_norm_g, w_out):
    bsz, seq, d = x.shape
    g = conv_pw_w.shape[-1]
    depth = w_in.shape[0]
    assert bsz == 2, "the long convolution packs exactly two batch elements as one complex sequence"
    assert seq % (FFT_INNER * DFT_RESIDUES) == 0
    n2 = FFT_INNER
    n1 = 2 * seq // n2
    n1f, n2f = FFT_INNER, seq // FFT_INNER

    tabs = {
        "fnet_outer": _outer_tables(n1f, n2f, False),
        "fnet_inner": _inner_table(n2f),
        "fnet_chan": _channel_table(g, 1.0 / math.sqrt(seq * g)),
        "hy_outer": _outer_tables(n1, n2, True),
        "filt_outer": _outer_tables(n1, n2, False),
        "inner": _inner_table(n2),
        "inv_inner": _inverse_inner_tables(n1, n2),
        "inv_outer": _inverse_outer_table(n1, n2),
    }
    tabs = {name: jnp.asarray(table) for name, table in tabs.items()}
    feats = _filter_features(positions, seq, n1, n2)
    mem_n = rmsnorm_bf16(mem.reshape(-1, d), mem_norm_g, min(512, mem.shape[0] * mem.shape[1]))

    w_in_all, w_out_all = w_in.astype(BF), w_out.astype(BF)
    x2 = x.reshape(bsz * seq, d)
    for l in range(depth):
        p = {
            "layer": l, "w_in_all": w_in_all, "w_out_all": w_out_all,
            "pre_g": pre_norm_g[l], "post_g": post_norm_g[l],
            "conv_dw_w": conv_dw_w[l], "conv_dw_b": conv_dw_b[l], "conv_ln_g": conv_ln_g[l],
            "conv_ln_b": conv_ln_b[l], "conv_pw_w": conv_pw_w[l], "conv_pw_b": conv_pw_b[l],
            "fnet_w": fnet_w[l], "fnet_b": fnet_b[l], "hy_short_w": hy_short_w[l], "hy_short_b": hy_short_b[l],
            "hy_fw1": hy_fw1[l], "hy_fb1": hy_fb1[l], "hy_freq1": hy_freq1[l], "hy_fw2": hy_fw2[l],
            "hy_fb2": hy_fb2[l], "hy_freq2": hy_freq2[l], "hy_fw3": hy_fw3[l], "hy_decay": hy_decay[l],
            "hy_skip": hy_skip[l], "mem_wk": mem_wk[l], "mem_wv": mem_wv[l], "group_g": group_norm_g[l],
        }
        x2 = _layer(x2, mem_n, feats, tabs, p, (bsz, seq, d, g))
    return x2.reshape(bsz, seq, d)
```

```python
import functools
import math

import jax
import jax.numpy as jnp
import numpy as np
from jax import lax
from jax.experimental import pallas as pl
from jax.experimental.pallas import tpu as pltpu

BF = jnp.bfloat16
F32 = jnp.float32
EPS = 1e-6

CONV_WIDTH = 31
CONV_HALF = CONV_WIDTH // 2
HALO = 16
SUBLANES, LANES = 8, 128
HYENA_BANDS = 16
HYENA_ORDER = 2
MEM_HEADS = 4
FFT_INNER = 128
FEAT_PAD = 128
FWD_COL = 2 * HYENA_BANDS + 1
BWD_COL = 2 * HYENA_BANDS + 2
DFT_RESIDUES = 16
MID_LANES = 256
VMEM_LIMIT = 56 * 1024 * 1024


def _params(*sem):
    return pltpu.CompilerParams(dimension_semantics=sem, vmem_limit_bytes=VMEM_LIMIT)


def _rmsnorm_kernel(x_ref, g_ref, o_ref):
    x = x_ref[...]
    ms = jnp.mean(x * x, axis=-1, keepdims=True)
    o_ref[...] = (x * lax.rsqrt(ms + EPS) * g_ref[...]).astype(o_ref.dtype)


def rmsnorm_bf16(x, g, tm):
    m, d = x.shape
    return pl.pallas_call(
        _rmsnorm_kernel,
        grid=(m // tm,),
        in_specs=[pl.BlockSpec((tm, d), lambda i: (i, 0)),
                  pl.BlockSpec((1, d), lambda i: (0, 0))],
        out_specs=pl.BlockSpec((tm, d), lambda i: (i, 0)),
        out_shape=jax.ShapeDtypeStruct((m, d), BF),
        compiler_params=_params("parallel"),
        name="rmsnorm",
    )(x, g.reshape(1, d))


def _mm_kernel(a_ref, w_ref, o_ref):
    o_ref[...] = jnp.dot(a_ref[...], w_ref[...], preferred_element_type=F32).astype(o_ref.dtype)


def matmul(a, w, tm, tn, name, col0=0, n=None):
    m, k = a.shape
    n = w.shape[1] if n is None else n
    tm, tn = min(tm, m), min(tn, n)
    joff = col0 // tn
    return pl.pallas_call(
        _mm_kernel,
        grid=(m // tm, n // tn),
        in_specs=[pl.BlockSpec((tm, k), lambda i, j: (i, 0)),
                  pl.BlockSpec((k, tn), lambda i, j: (0, j + joff))],
        out_specs=pl.BlockSpec((tm, tn), lambda i, j: (i, j)),
        out_shape=jax.ShapeDtypeStruct((m, n), BF),
        compiler_params=_params("parallel", "arbitrary"),
        name=name,
    )(a, w)


def _proj_kernel(x_ref, g_ref, w_ref, o_ref, h_s):
    @pl.when(pl.program_id(1) == 0)
    def _():
        x = x_ref[...]
        ms = jnp.mean(x * x, axis=-1, keepdims=True)
        h_s[...] = (x * lax.rsqrt(ms + EPS) * g_ref[...]).astype(h_s.dtype)

    o_ref[...] = jnp.dot(h_s[...], w_ref[...], preferred_element_type=F32).astype(o_ref.dtype)


def input_projection(x, g, w, layer, tm, tn):
    m, d = x.shape
    n = w.shape[2]
    return pl.pallas_call(
        _proj_kernel,
        grid=(m // tm, n // tn),
        in_specs=[pl.BlockSpec((tm, d), lambda i, j: (i, 0)),
                  pl.BlockSpec((1, d), lambda i, j: (0, 0)),
                  pl.BlockSpec((None, d, tn), lambda i, j: (layer, 0, j))],
        out_specs=pl.BlockSpec((tm, tn), lambda i, j: (i, j)),
        out_shape=jax.ShapeDtypeStruct((m, n), BF),
        scratch_shapes=[pltpu.VMEM((tm, d), BF)],
        compiler_params=_params("parallel", "arbitrary"),
        name="input_projection",
    )(x, g.reshape(1, d), w)


def _halo_specs(tr, width, col):
    per = tr // HALO

    def cur(i):
        return (i, col)

    def prev(i):
        return (jnp.maximum(i * per - 1, 0), col)

    def make_next(nblocks):
        def nxt(i):
            return (jnp.minimum((i + 1) * per, nblocks - 1), col)
        return nxt

    return cur, prev, make_next


def _seq_edges(i, tr, seq):
    tiles = seq // tr
    pos = i % tiles
    return pos == 0, pos == tiles - 1


def _conva_kernel(vc, gc, vp, gp, vn, gn, dww, dwb, lng, lnb, pww, pwb, o_ref, upad, cbuf, wbc, *, seq):
    tr, c = vc.shape
    first, last = _seq_edges(pl.program_id(0), tr, seq)

    def glu(v, g):
        return v[...].astype(F32) * jax.nn.sigmoid(g[...].astype(F32))

    upad[pl.ds(HALO, tr), :] = glu(vc, gc)
    upad[pl.ds(0, HALO), :] = jnp.where(first, 0.0, glu(vp, gp))
    upad[pl.ds(HALO + tr, HALO), :] = jnp.where(last, 0.0, glu(vn, gn))

    for j in range(CONV_WIDTH):
        wbc[pl.ds(SUBLANES * j, SUBLANES), :] = jnp.broadcast_to(dww[pl.ds(j, 1), :], (SUBLANES, c))
    row = lax.broadcasted_iota(jnp.int32, (SUBLANES, LANES), 0)
    for ct in range(c // LANES):
        lanes = pl.ds(ct * LANES, LANES)

        def residue_sums(b, lanes=lanes):
            u = [upad[pl.ds(pl.multiple_of((b + a) * SUBLANES, SUBLANES), SUBLANES), lanes] for a in range(4)]
            sums = []
            for s in range(SUBLANES):
                acc = None
                for a in range(4):
                    j = SUBLANES * a + s - 1
                    if 0 <= j < CONV_WIDTH:
                        term = u[a] * wbc[pl.ds(SUBLANES * j, SUBLANES), lanes]
                        acc = term if acc is None else acc + term
                sums.append(acc)
            return tuple(sums)

        bias = jnp.broadcast_to(dwb[:, lanes], (SUBLANES, LANES))

        def body(b, prev, lanes=lanes, bias=bias, residue_sums=residue_sums):
            cur = residue_sums(b)
            terms = [bias + prev[0]]
            for s in range(1, SUBLANES):
                terms.append(pltpu.roll(jnp.where(row >= s, prev[s], cur[s]), SUBLANES - s, axis=0))
            while len(terms) > 1:
                terms = [terms[i] + terms[i + 1] for i in range(0, len(terms), 2)]
            out = terms[0]
            cbuf[pl.ds(pl.multiple_of((b - 1) * SUBLANES, SUBLANES), SUBLANES), lanes] = out
            return cur

        lax.fori_loop(1, tr // SUBLANES + 1, body, residue_sums(0), unroll=4)

    u = cbuf[...]
    mu = jnp.mean(u, axis=-1, keepdims=True)
    d = u - mu
    var = jnp.mean(d * d, axis=-1, keepdims=True)
    y = d * lax.rsqrt(var + EPS) * lng[...] + lnb[...]
    y = y * jax.nn.sigmoid(y)
    out = jnp.dot(y.astype(BF), pww[...], preferred_element_type=F32) + pwb[...]
    o_ref[...] = out.astype(o_ref.dtype)


def conv_module(a, dww, dwb, lng, lnb, pww, pwb, seq, tr):
    m = a.shape[0]
    c = pww.shape[0]
    cur, prev, make_next = _halo_specs(tr, c, 0)
    cur_g, prev_g, make_next_g = _halo_specs(tr, c, 1)
    nh = m // HALO
    row = lambda i: (0, 0)
    return pl.pallas_call(
        functools.partial(_conva_kernel, seq=seq),
        grid=(m // tr,),
        in_specs=[pl.BlockSpec((tr, c), cur), pl.BlockSpec((tr, c), cur_g),
                  pl.BlockSpec((HALO, c), prev), pl.BlockSpec((HALO, c), prev_g),
                  pl.BlockSpec((HALO, c), make_next(nh)), pl.BlockSpec((HALO, c), make_next_g(nh)),
                  pl.BlockSpec((CONV_WIDTH, c), row), pl.BlockSpec((1, c), row),
                  pl.BlockSpec((1, c), row), pl.BlockSpec((1, c), row),
                  pl.BlockSpec((c, c), row), pl.BlockSpec((1, c), row)],
        out_specs=pl.BlockSpec((tr, c), lambda i: (i, 0)),
        out_shape=jax.ShapeDtypeStruct((m, c), BF),
        scratch_shapes=[pltpu.VMEM((tr + 2 * HALO, c), F32), pltpu.VMEM((tr, c), F32),
                        pltpu.VMEM((CONV_WIDTH * SUBLANES, c), F32)],
        compiler_params=_params("parallel"),
        name="conv_module",
    )(a, a, a, a, a, a, dww, dwb.reshape(1, c), lng.reshape(1, c), lnb.reshape(1, c),
      pww.astype(BF), pwb.reshape(1, c))


def _short_kernel(hc, hp, hn, w, b, v_ref, x1_ref, x2_ref, *, seq):
    tr, c3 = hc.shape
    c = c3 // 3
    first, last = _seq_edges(pl.program_id(0), tr, seq)
    edge = lax.broadcasted_iota(jnp.int32, (HALO, c), 0)
    for g, o_ref in enumerate((v_ref, x1_ref, x2_ref)):
        cols = pl.ds(g * c, c)
        w0, w1, w2, bias = w[pl.ds(0, 1), cols], w[pl.ds(1, 1), cols], w[pl.ds(2, 1), cols], b[:, cols]
        cur = hc[:, cols].astype(F32)
        before = pltpu.roll(cur, 1, axis=0)
        after = pltpu.roll(cur, tr - 1, axis=0)
        o_ref[...] = (before * w0 + cur * w1 + after * w2 + bias).astype(o_ref.dtype)

        prev_row = jnp.where(first, 0.0, hp[pl.ds(HALO - 1, 1), cols].astype(F32))
        next_row = jnp.where(last, 0.0, hn[pl.ds(0, 1), cols].astype(F32))
        head = jnp.where(edge == 0, prev_row, before[:HALO])
        tail = jnp.where(edge == HALO - 1, next_row, after[tr - HALO:])
        o_ref[pl.ds(0, HALO), :] = (head * w0 + cur[:HALO] * w1 + after[:HALO] * w2 + bias).astype(o_ref.dtype)
        o_ref[pl.ds(tr - HALO, HALO), :] = (before[tr - HALO:] * w0 + cur[tr - HALO:] * w1 + tail * w2
                                            + bias).astype(o_ref.dtype)


def hyena_short_conv(hy, w, b, seq, tr, col):
    m = hy.shape[0]
    c3 = w.shape[1]
    c = c3 // 3
    cur, prev, make_next = _halo_specs(tr, c3, col)
    out = jax.ShapeDtypeStruct((m, c), BF)
    ospec = pl.BlockSpec((tr, c), lambda i: (i, 0))
    return pl.pallas_call(
        functools.partial(_short_kernel, seq=seq),
        grid=(m // tr,),
        in_specs=[pl.BlockSpec((tr, c3), cur), pl.BlockSpec((HALO, c3), prev),
                  pl.BlockSpec((HALO, c3), make_next(m // HALO)),
                  pl.BlockSpec((3, c3), lambda i: (0, 0)), pl.BlockSpec((1, c3), lambda i: (0, 0))],
        out_specs=[ospec, ospec, ospec],
        out_shape=[out, out, out],
        compiler_params=_params("parallel"),
        name="hyena_short_conv",
    )(hy, hy, hy, w, b.reshape(1, c3))


def _cis(p, n, sign):
    ang = (2.0 * np.pi / n) * (p % n)
    return np.cos(ang), sign * np.sin(ang)


def _stack_complex(mr, mi):
    return np.concatenate([np.concatenate([mr, -mi], axis=-1),
                           np.concatenate([mi, mr], axis=-1)], axis=-2)


def _bf16_const(a):
    return a.astype(jnp.bfloat16)


def _outer_stage_kernel(x_ref, t_ref, or_ref, oi_ref, yr_s, yi_s, *, nb):
    xt = pltpu.einshape("nrc->rnc", x_ref[...])
    for r in range(nb):
        y = jnp.dot(t_ref[r], xt[r], preferred_element_type=F32)
        h = y.shape[0] // 2
        yr_s[r] = y[:h].astype(BF)
        yi_s[r] = y[h:].astype(BF)
    or_ref[...] = pltpu.einshape("rnc->nrc", yr_s[...])
    oi_ref[...] = pltpu.einshape("rnc->nrc", yi_s[...])


def dft_outer_stage(x, table, nb, tc, name, c=None, col0=0):
    bsz, rows, n_inner, _ = x.shape
    c = x.shape[3] if c is None else c
    _, two_k, _ = table.shape
    kout = two_k // 2
    tc = min(tc, c)
    coff = col0 // tc
    out = jax.ShapeDtypeStruct((bsz, kout, n_inner, c), BF)
    ospec = pl.BlockSpec((None, kout, nb, tc), lambda b, j, cc: (b, 0, j, cc))
    return pl.pallas_call(
        functools.partial(_outer_stage_kernel, nb=nb),
        grid=(bsz, n_inner // nb, c // tc),
        in_specs=[pl.BlockSpec((None, rows, nb, tc), lambda b, j, cc: (b, 0, j, cc + coff)),
                  pl.BlockSpec((nb, two_k, rows), lambda b, j, cc: (j, 0, 0))],
        out_specs=[ospec, ospec],
        out_shape=[out, out],
        scratch_shapes=[pltpu.VMEM((nb, kout, tc), BF), pltpu.VMEM((nb, kout, tc), BF)],
        compiler_params=_params("parallel", "arbitrary", "arbitrary"),
        name=name,
    )(x, table)


def _filter_inner_kernel(ur, ui, d, ss, kr_ref, ki_ref, *, kb, n2):
    scale = lax.rsqrt(ss[...] + EPS)
    urt = pltpu.einshape("nkc->knc", ur[...])
    uit = pltpu.einshape("nkc->knc", ui[...])
    for k in range(kb):
        rows = pl.ds(k * n2, n2)
        cat = jnp.concatenate([urt[k], uit[k]], axis=0)
        x = jnp.dot(d[...], cat, preferred_element_type=F32) * scale
        kr_ref[rows, :] = x[:n2].astype(kr_ref.dtype)
        ki_ref[rows, :] = x[n2:].astype(ki_ref.dtype)


def filter_inner_stage(ur, ui, d, ss, kb, tc):
    n2, n1, c = ur.shape
    tc = min(tc, c)
    ispec = pl.BlockSpec((n2, kb, tc), lambda i, cc: (0, i, cc))
    ospec = pl.BlockSpec((kb * n2, tc), lambda i, cc: (i, cc))
    out = jax.ShapeDtypeStruct((n1 * n2, c), BF)
    return pl.pallas_call(
        functools.partial(_filter_inner_kernel, kb=kb, n2=n2),
        grid=(n1 // kb, c // tc),
        in_specs=[ispec, ispec, pl.BlockSpec((2 * n2, 2 * n2), lambda i, cc: (0, 0)),
                  pl.BlockSpec((1, tc), lambda i, cc: (0, cc))],
        out_specs=[ospec, ospec],
        out_shape=[out, out],
        compiler_params=_params("parallel", "arbitrary"),
        name="filter_inner_dft",
    )(ur, ui, d, ss)


def _hyena_mid_kernel(ur, ui, kr, ki, d, minv, qr_ref, qi_ref, *, kb, n2):
    c = ur.shape[1]
    cw = min(c, MID_LANES)
    for k in range(kb):
        rows = pl.ds(k * n2, n2)
        for c0 in range(0, c, cw):
            cols = pl.ds(c0, cw)
            cat = jnp.concatenate([ur[rows, cols], ui[rows, cols]], axis=0)
            x = jnp.dot(d[...], cat, preferred_element_type=F32)
            xr, xi = x[:n2], x[n2:]
            fr, fi = kr[rows, cols].astype(F32), ki[rows, cols].astype(F32)
            pr = xr * fr - xi * fi
            pi = xr * fi + xi * fr
            catp = jnp.concatenate([pr, pi], axis=0).astype(BF)
            q = jnp.dot(minv[k], catp, preferred_element_type=F32)
            qr_ref[rows, cols] = q[:n2].astype(qr_ref.dtype)
            qi_ref[rows, cols] = q[n2:].astype(qi_ref.dtype)


def hyena_mid_stage(ur, ui, kfr, kfi, d, minv, order, n2, kb):
    n, c = ur.shape
    n1 = n // n2
    kb = min(kb, n1)
    spec = pl.BlockSpec((kb * n2, c), lambda i: (i, 0))
    kspec = pl.BlockSpec((kb * n2, c), lambda i: (i, order))
    out = jax.ShapeDtypeStruct((n, c), BF)
    return pl.pallas_call(
        functools.partial(_hyena_mid_kernel, kb=kb, n2=n2),
        grid=(n1 // kb,),
        in_specs=[spec, spec, kspec, kspec,
                  pl.BlockSpec((2 * n2, 2 * n2), lambda i: (0, 0)),
                  pl.BlockSpec((kb, 2 * n2, 2 * n2), lambda i: (i, 0, 0))],
        out_specs=[spec, spec],
        out_shape=[out, out],
        compiler_params=_params("parallel"),
        name="hyena_mid",
    )(ur, ui, kfr, kfi, d, minv)


def _hyena_out_kernel(qr, qi, g, xo, z, skip, o_ref, y_s, *, nb):
    qrt = pltpu.einshape("nrc->rnc", qr[...])
    qit = pltpu.einshape("nrc->rnc", qi[...])
    for r in range(nb):
        cat = jnp.concatenate([qrt[r], qit[r]], axis=0)
        y_s[r] = jnp.dot(g[...], cat, preferred_element_type=F32)
    y = pltpu.einshape("rnc->nrc", y_s[...])
    zf = z[...].astype(F32)
    o_ref[...] = (xo[...].astype(F32) * (y + skip[...] * zf)).astype(o_ref.dtype)


def hyena_out_stage(qr, qi, g, xo, z, skip, nb, tc):
    rows, n2, c = qr.shape
    tc = min(tc, c)
    spec = pl.BlockSpec((rows, nb, tc), lambda j, cc: (0, j, cc))
    return pl.pallas_call(
        functools.partial(_hyena_out_kernel, nb=nb),
        grid=(n2 // nb, c // tc),
        in_specs=[spec, spec, pl.BlockSpec((rows, 2 * rows), lambda j, cc: (0, 0)), spec, spec,
                  pl.BlockSpec((1, tc), lambda j, cc: (0, cc))],
        out_specs=spec,
        out_shape=jax.ShapeDtypeStruct((rows, n2, c), BF),
        scratch_shapes=[pltpu.VMEM((nb, rows, tc), F32)],
        compiler_params=_params("parallel", "arbitrary"),
        name="hyena_out",
    )(qr, qi, g, xo, z, skip)


def _dot_bf16x3(a, b):
    ah = a.astype(BF)
    bh = b.astype(BF)
    al = (a - ah.astype(F32)).astype(BF)
    bl = (b - bh.astype(F32)).astype(BF)
    return jnp.dot(jnp.concatenate([ah, al, ah], axis=1), jnp.concatenate([bh, bh, bl], axis=0),
                   preferred_element_type=F32)


def _filter_kernel(ft, w1, b1, f1, w2, b2, f2, w3, decf, decb, tab, ur_ref, ui_ref, ss_ref, *, n1):
    hi = lax.Precision.HIGHEST
    x = ft[...]
    ffn2 = w2.shape[1]
    h = jnp.sin(f1[...] * (jnp.dot(x, w1[...], precision=hi, preferred_element_type=F32) + b1[...]))
    h = jnp.sin(f2[...] * (jnp.dot(h, w2[...], precision=hi, preferred_element_type=F32) + b2[...]))
    t = x[:, 0:1]
    is_f = x[:, FWD_COL:FWD_COL + 1]
    is_b = x[:, BWD_COL:BWD_COL + 1]
    lane = lax.broadcasted_iota(jnp.int32, h.shape, 1)
    h = h * jnp.where(lane < ffn2 // 2, is_f, is_b)
    h = _dot_bf16x3(h, w3[...])
    dec = jnp.where(is_f > 0.0, jnp.abs(decf[...]), jnp.abs(decb[...]))
    k = h * jnp.exp(-t * dec)
    kb = k.astype(BF)
    for r in range(tab.shape[0]):
        rows = pl.ds(r * n1, n1)
        u = jnp.dot(tab[r], kb[r * n1:(r + 1) * n1], preferred_element_type=F32)
        ur_ref[rows, :] = u[:n1].astype(ur_ref.dtype)
        ui_ref[rows, :] = u[n1:].astype(ui_ref.dtype)
    part = jnp.sum(k * k, axis=0, keepdims=True)

    @pl.when(pl.program_id(0) == 0)
    def _():
        ss_ref[...] = jnp.zeros_like(ss_ref)

    ss_ref[...] += jnp.broadcast_to(part, ss_ref.shape)


def hyena_filter_taps(feats, w1, b1, f1, w2, b2, f2, w3, decf, decb, table, nb):
    n, fp = feats.shape
    ffn = w1.shape[1]
    ffn2 = w2.shape[1]
    oc = w3.shape[-1]
    _, two_n1, n1 = table.shape
    tr = nb * n1
    small = lambda shape: pl.BlockSpec(shape, lambda i: (0,) * len(shape))
    ospec = pl.BlockSpec((tr, oc), lambda i: (i, 0))
    out = jax.ShapeDtypeStruct((n, oc), BF)
    return pl.pallas_call(
        functools.partial(_filter_kernel, n1=n1),
        grid=(n // tr,),
        in_specs=[pl.BlockSpec((tr, fp), lambda i: (i, 0)),
                  small((fp, ffn)), small((1, ffn)), small((1, ffn)),
                  small((ffn, ffn2)), small((1, ffn2)), small((1, ffn2)),
                  small((ffn2, oc)), small((1, oc)), small((1, oc)),
                  pl.BlockSpec((nb, two_n1, n1), lambda i: (i, 0, 0))],
        out_specs=[ospec, ospec, pl.BlockSpec((8, oc), lambda i: (0, 0))],
        out_shape=[out, out, jax.ShapeDtypeStruct((8, oc), F32)],
        compiler_params=_params("arbitrary"),
        name="hyena_filter_taps",
    )(feats, w1, b1, f1, w2, b2, f2, w3, decf, decb, table)


def _fnet_kernel(yr, yi, d2, csgw, b, o_ref, xcat, o_s, *, kb, n2, c):
    for k in range(kb):
        rows = pl.ds(k * n2, n2)
        cat = jnp.concatenate([yr[rows, :], yi[rows, :]], axis=0)
        x = jnp.dot(d2[...], cat, preferred_element_type=F32)
        xcat[rows, pl.ds(0, c)] = x[:n2].astype(BF)
        xcat[rows, pl.ds(c, c)] = x[n2:].astype(BF)
    out = jnp.dot(xcat[...], csgw[...], preferred_element_type=F32) + b[...]
    for k in range(kb):
        o_s[k] = out[k * n2:(k + 1) * n2].astype(o_s.dtype)
    o_ref[...] = pltpu.einshape("kmc->mkc", o_s[...])


def fnet_inner_stage(yr, yi, d2, csgw, b, n2, kb):
    bsz, l, c = yr.shape
    n1 = l // n2
    kb = min(kb, n1)
    spec = pl.BlockSpec((None, kb * n2, c), lambda bb, i: (bb, i, 0))
    const = lambda shape: pl.BlockSpec(shape, lambda bb, i: (0,) * len(shape))
    return pl.pallas_call(
        functools.partial(_fnet_kernel, kb=kb, n2=n2, c=c),
        grid=(bsz, n1 // kb),
        in_specs=[spec, spec, const((2 * n2, 2 * n2)), const((2 * c, c)), const((1, c))],
        out_specs=pl.BlockSpec((None, n2, kb, c), lambda bb, i: (bb, 0, i, 0)),
        out_shape=jax.ShapeDtypeStruct((bsz, n2, n1, c), BF),
        scratch_shapes=[pltpu.VMEM((kb * n2, 2 * c), BF), pltpu.VMEM((kb, n2, c), BF)],
        compiler_params=_params("parallel", "arbitrary"),
        name="fnet_inner",
    )(yr, yi, d2, csgw, b)


def _attn_kernel(q_ref, k_ref, v_ref, o_ref, *, heads):
    c = q_ref.shape[1]
    hd = c // heads
    scale = hd ** -0.5
    for h in range(heads):
        cols = pl.ds(h * hd, hd)
        s = lax.dot_general(q_ref[:, cols], k_ref[:, cols], (((1,), (1,)), ((), ())),
                            preferred_element_type=F32) * scale
        s = s - jnp.max(s, axis=-1, keepdims=True)
        e = jnp.exp(s)
        p = e / jnp.sum(e, axis=-1, keepdims=True)
        o = jnp.dot(p.astype(BF), v_ref[:, cols], preferred_element_type=F32)
        o_ref[:, cols] = o.astype(o_ref.dtype)


def memory_attention(q, kh, vh, seq, tr, col):
    m = q.shape[0]
    _, nm, c = kh.shape
    per = seq // tr
    kv = pl.BlockSpec((None, nm, c), lambda i: (i // per, 0, 0))
    return pl.pallas_call(
        functools.partial(_attn_kernel, heads=MEM_HEADS),
        grid=(m // tr,),
        in_specs=[pl.BlockSpec((tr, c), lambda i: (i, col)), kv, kv],
        out_specs=pl.BlockSpec((tr, c), lambda i: (i, 0)),
        out_shape=jax.ShapeDtypeStruct((m, c), BF),
        compiler_params=_params("parallel"),
        name="memory_attention",
    )(q, kh, vh)


def _merge_kernel(ya, yb, yc, ym, ga, gb, gc, gm, gg, w, x, pg, o_ref):
    g = ya.shape[1]
    acc = None
    for idx, (y_ref, gate) in enumerate(((ya, ga), (yb, gb), (yc, gc), (ym, gm))):
        cols = pl.ds(idx * g, g)
        y = y_ref[...].astype(F32)
        ms = jnp.mean(y * y, axis=-1, keepdims=True)
        yn = y * lax.rsqrt(ms + EPS) * gg[:, cols]
        gt = gate[...]
        yn = yn.astype(BF) * (gt * jax.nn.sigmoid(gt))
        part = jnp.dot(yn, w[cols, :], preferred_element_type=F32)
        acc = part if acc is None else acc + part
    ms = jnp.mean(acc * acc, axis=-1, keepdims=True)
    o_ref[...] = x[...] + acc * lax.rsqrt(ms + EPS) * pg[...]


def merge_project(ya, yb, yc, ym, gate, gate_col, gg, w_out, layer, x, pg, tm):
    m, g = ya.shape
    d = x.shape[1]
    yspec = pl.BlockSpec((tm, g), lambda i: (i, 0))
    gspecs = [pl.BlockSpec((tm, g), lambda i, col=gate_col + k: (i, col)) for k in range(4)]
    const = lambda shape: pl.BlockSpec(shape, lambda i: (0, 0))
    return pl.pallas_call(
        _merge_kernel,
        grid=(m // tm,),
        in_specs=[yspec, yspec, yspec, yspec, *gspecs,
                  const((1, 4 * g)),
                  pl.BlockSpec((None, 4 * g, d), lambda i: (layer, 0, 0), pipeline_mode=pl.Buffered(1)),
                  pl.BlockSpec((tm, d), lambda i: (i, 0)),
                  const((1, d))],
        out_specs=pl.BlockSpec((tm, d), lambda i: (i, 0)),
        out_shape=jax.ShapeDtypeStruct((m, d), F32),
        compiler_params=_params("parallel"),
        name="merge_project",
    )(ya, yb, yc, ym, gate, gate, gate, gate, gg, w_out, x, pg)


def _iota(n):
    return np.arange(n, dtype=np.int64)


@functools.lru_cache(maxsize=None)
def _outer_tables(n1, n2, rows_complex):
    n = n1 * n2
    r = _iota(n2)[:, None, None]
    k1 = _iota(n1)[None, :, None]
    nj = n1 // 2 if rows_complex else n1
    j = _iota(nj)[None, None, :]
    er, ei = _cis(k1 * (n2 * j + r), n, -1.0)
    if rows_complex:
        return _bf16_const(_stack_complex(er, ei))
    return _bf16_const(np.concatenate([er, ei], axis=-2))


@functools.lru_cache(maxsize=None)
def _inner_table(n2):
    k = _iota(n2)[:, None]
    j = _iota(n2)[None, :]
    cr, ci = _cis(k * j, n2, -1.0)
    return _bf16_const(_stack_complex(cr, ci))


@functools.lru_cache(maxsize=None)
def _inverse_inner_tables(n1, n2):
    n = n1 * n2
    k1 = _iota(n1)[:, None, None]
    na = _iota(n2)[None, :, None]
    k2 = _iota(n2)[None, None, :]
    ar, ai = _cis(na * (k1 + n1 * k2), n, 1.0)
    return _bf16_const(_stack_complex(ar, ai))


@functools.lru_cache(maxsize=None)
def _inverse_outer_table(n1, n2):
    nb = _iota(n1 // 2)[:, None]
    k1 = _iota(n1)[None, :]
    gr, gi = _cis(nb * k1, n1, 1.0)
    inv = 1.0 / (n1 * n2)
    return _bf16_const(_stack_complex(gr * inv, gi * inv))


@functools.lru_cache(maxsize=None)
def _channel_table(g, scale):
    a = _iota(g)[:, None]
    b = _iota(g)[None, :]
    cg, sg = _cis(a * b, g, 1.0)
    return _bf16_const(np.concatenate([cg, sg], axis=0) * scale)


def _filter_features(positions, seq, n1, n2):
    idx = (n2 * _iota(n1)[None, :] + _iota(n2)[:, None]).reshape(-1)
    pos = positions.astype(F32)
    pos = jnp.concatenate([pos, pos[:1], jnp.flip(pos[1:])])
    pos = pos.reshape(n1, n2).T.reshape(-1)
    t = pos / seq
    bands = jnp.linspace(1e-4, HYENA_BANDS - 1, HYENA_BANDS, dtype=F32)
    w = 2.0 * jnp.pi * pos / seq
    is_f = (idx < seq).astype(F32)
    is_b = (idx > seq).astype(F32)
    feats = jnp.concatenate([t[:, None], jnp.cos(w[:, None] * bands), jnp.sin(w[:, None] * bands),
                             is_f[:, None], is_b[:, None]], axis=-1)
    return jnp.pad(feats, ((0, 0), (0, FEAT_PAD - feats.shape[1])))


def _layer(x2, mem_n, feats, tabs, p, dims):
    bsz, seq, d, g = dims
    m = bsz * seq
    n2 = FFT_INNER
    n1 = 2 * seq // n2
    n1f, n2f = FFT_INNER, seq // FFT_INNER

    proj = input_projection(x2, p["pre_g"], p["w_in_all"], p["layer"], 1024, g)
    n_in = proj.shape[1]

    y_a = conv_module(proj, p["conv_dw_w"], p["conv_dw_b"], p["conv_ln_g"], p["conv_ln_b"],
                      p["conv_pw_w"], p["conv_pw_b"], seq, 512)

    yr, yi = dft_outer_stage(proj.reshape(bsz, n1f, n2f, n_in), tabs["fnet_outer"], DFT_RESIDUES, 512,
                             "fnet_outer", c=g, col0=2 * g)
    csgw = matmul(tabs["fnet_chan"], p["fnet_w"].astype(BF), 1024, 1024, "fnet_fold")
    y_b = fnet_inner_stage(yr.reshape(bsz, seq, g), yi.reshape(bsz, seq, g), tabs["fnet_inner"],
                           csgw, p["fnet_b"].reshape(1, g), n2f, 16)
    y_b = y_b.reshape(m, g)

    v, x1, x2g = hyena_short_conv(proj, p["hy_short_w"], p["hy_short_b"], seq, 512, 1)
    ffn = p["hy_fw1"].shape[1]
    og = HYENA_ORDER * g
    w1 = jnp.pad(p["hy_fw1"], ((0, FEAT_PAD - p["hy_fw1"].shape[0]), (0, 0)))
    dup = lambda a: jnp.concatenate([a, a], axis=-1)
    w3 = p["hy_fw3"].reshape(ffn, HYENA_ORDER, 2, g)
    w3 = jnp.concatenate([w3[:, :, 0, :].reshape(ffn, og), w3[:, :, 1, :].reshape(ffn, og)], axis=0)
    kur, kui, ss = hyena_filter_taps(
        feats, w1, p["hy_fb1"].reshape(1, ffn), p["hy_freq1"].reshape(1, ffn),
        dup(p["hy_fw2"]), dup(p["hy_fb2"].reshape(1, ffn)), dup(p["hy_freq2"].reshape(1, ffn)),
        w3, p["hy_decay"][:, 0, :].reshape(1, og), p["hy_decay"][:, 1, :].reshape(1, og),
        tabs["filt_outer"], max(1, 512 // n1))
    kfr, kfi = filter_inner_stage(kur.reshape(n2, n1, og), kui.reshape(n2, n1, og), tabs["inner"], ss[0:1],
                                  DFT_RESIDUES, 1024)

    z = v
    for o, gate_o in enumerate((x1, x2g)):
        ur, ui = dft_outer_stage(z.reshape(1, n1, n2, g), tabs["hy_outer"], DFT_RESIDUES, 512, "hyena_outer_dft")
        qr, qi = hyena_mid_stage(ur.reshape(n1 * n2, g), ui.reshape(n1 * n2, g), kfr, kfi,
                                 tabs["inner"], tabs["inv_inner"], o, n2, 8)
        z = hyena_out_stage(qr.reshape(n1, n2, g), qi.reshape(n1, n2, g), tabs["inv_outer"],
                            gate_o.reshape(n1, n2, g), z.reshape(n1, n2, g), p["hy_skip"][o].reshape(1, g),
                            DFT_RESIDUES, 512)
    y_c = z.reshape(m, g)

    nm = mem_n.shape[0] // bsz
    kh = matmul(mem_n, p["mem_wk"].astype(BF), 512, 1024, "mem_k").reshape(bsz, nm, g)
    vh = matmul(mem_n, p["mem_wv"].astype(BF), 512, 1024, "mem_v").reshape(bsz, nm, g)
    y_m = memory_attention(proj, kh, vh, seq, 1024, 6)

    return merge_project(y_a, y_b, y_c, y_m, proj, 7, p["group_g"].reshape(1, 4 * g), p["w_out_all"], p["layer"],
                         x2, p["post_g"].reshape(1, d), 256)


def kernel(x, mem, positions, mem_norm_g, pre_norm_g, post_norm_g, w_in, conv_dw_w, conv_dw_b, conv_ln_g, conv_ln_b, conv_pw_w, conv_pw_b, fnet_w, fnet_b, hy_short_w, hy_short_b, hy_fw1, hy_fb1, hy_freq1, hy_fw2, hy_fb2, hy_freq2, hy_fw3, hy_decay, hy_skip, mem_wk, mem_wv, group_norm_g, w_out):
    bsz, seq, d = x.shape
    g = conv_pw_w.shape[-1]
    depth = w_in.shape[0]
    assert bsz == 2, "the long convolution packs exactly two batch elements as one complex sequence"
    assert seq % (FFT_INNER * DFT_RESIDUES) == 0
    n2 = FFT_INNER
    n1 = 2 * seq // n2
    n1f, n2f = FFT_INNER, seq // FFT_INNER

    tabs = {
        "fnet_outer": _outer_tables(n1f, n2f, False),
        "fnet_inner": _inner_table(n2f),
        "fnet_chan": _channel_table(g, 1.0 / math.sqrt(seq * g)),
        "hy_outer": _outer_tables(n1, n2, True),
        "filt_outer": _outer_tables(n1, n2, False),
        "inner": _inner_table(n2),
        "inv_inner": _inverse_inner_tables(n1, n2),
        "inv_outer": _inverse_outer_table(n1, n2),
    }
    tabs = {name: jnp.asarray(table) for name, table in tabs.items()}
    feats = _filter_features(positions, seq, n1, n2)
    mem_n = rmsnorm_bf16(mem.reshape(-1, d), mem_norm_g, min(512, mem.shape[0] * mem.shape[1]))

    w_in_all, w_out_all = w_in.astype(BF), w_out.astype(BF)
    x2 = x.reshape(bsz * seq, d)
    for l in range(depth):
        p = {
            "layer": l, "w_in_all": w_in_all, "w_out_all": w_out_all,
            "pre_g": pre_norm_g[l], "post_g": post_norm_g[l],
            "conv_dw_w": conv_dw_w[l], "conv_dw_b": conv_dw_b[l], "conv_ln_g": conv_ln_g[l],
            "conv_ln_b": conv_ln_b[l], "conv_pw_w": conv_pw_w[l], "conv_pw_b": conv_pw_b[l],
            "fnet_w": fnet_w[l], "fnet_b": fnet_b[l], "hy_short_w": hy_short_w[l], "hy_short_b": hy_short_b[l],
            "hy_fw1": hy_fw1[l], "hy_fb1": hy_fb1[l], "hy_freq1": hy_freq1[l], "hy_fw2": hy_fw2[l],
            "hy_fb2": hy_fb2[l], "hy_freq2": hy_freq2[l], "hy_fw3": hy_fw3[l], "hy_decay": hy_decay[l],
            "hy_skip": hy_skip[l], "mem_wk": mem_wk[l], "mem_wv": mem_wv[l], "group_g": group_norm_g[l],
        }
        x2 = _layer(x2, mem_n, feats, tabs, p, (bsz, seq, d, g))
    return x2.reshape(bsz, seq, d)
```
